```python
import jax, jax.numpy as jnp
from jax import lax
import numpy as np

D_MODEL = 1024
BATCH = 32
SEQ = 256
DEPTH = 1
DEC_BATCH = 8
DEC_SEQ = 1024
PAST_LEN = 512

GRID_W = 64
EPS = 1e-6
A_HEADS = 4
A_DK = 128
A_DV = 128
A_WIDTH = A_HEADS * A_DV
A_CHUNK = 16
B_HEADS = 8
B_KV_HEADS = 2
B_HEAD_DIM = 64
B_WIDTH = B_HEADS * B_HEAD_DIM
WINDOW = 128
B_BLOCK = 128
ROPE_BASE = 10000.0
P_HEADS = 8
P_NKEYS = 128
P_EXPERTS = P_NKEYS * P_NKEYS
P_QDIM = 256
P_TOPK = 16
P_TOK_BLOCK = 128
SEG_SIZES = (A_HEADS * A_DK, A_WIDTH, A_HEADS * A_DK, A_HEADS * A_DK, A_WIDTH,
             B_WIDTH, B_KV_HEADS * B_HEAD_DIM, B_KV_HEADS * B_HEAD_DIM, D_MODEL, D_MODEL)
D_IN = sum(SEG_SIZES)

kernel_name = 'hybrid_hgrn2_swa_peer_dit_step'


def rmsnorm(x, g):
    xf = x.astype(jnp.float32)
    y = xf * lax.rsqrt(jnp.mean(xf * xf, axis=-1, keepdims=True) + EPS)
    return (y * g.astype(jnp.float32)).astype(x.dtype)


def adaln(cond, w_ada, b_ada):
    m = (jax.nn.silu(cond) @ w_ada + b_ada)[:, None, :]
    return jnp.split(m, 6, axis=-1)


def split_columns(p):
    offs = np.cumsum(SEG_SIZES)[:-1].tolist()
    return jnp.split(p, offs, axis=-1)


def split_heads(x, h):
    return x.reshape(*x.shape[:-1], h, x.shape[-1] // h)


def axial_rope(x):
    T = x.shape[1]
    rows_n = T // GRID_W
    row = jnp.repeat(jnp.arange(rows_n), GRID_W).astype(jnp.float32)
    col = jnp.tile(jnp.arange(GRID_W), rows_n).astype(jnp.float32)
    half = x.shape[-1] // 2
    nf = half // 2
    inv = ROPE_BASE ** (-jnp.arange(nf, dtype=jnp.float32) / nf)

    def rot(xp, pos):
        ang = pos[:, None] * inv[None, :]
        cos = jnp.cos(ang)[None, :, None, :]
        sin = jnp.sin(ang)[None, :, None, :]
        x1 = xp[..., :nf].astype(jnp.float32)
        x2 = xp[..., nf:].astype(jnp.float32)
        return jnp.concatenate([x1 * cos - x2 * sin, x2 * cos + x1 * sin], axis=-1)

    return jnp.concatenate([rot(x[..., :half], row), rot(x[..., half:], col)], axis=-1).astype(x.dtype)


def hgrn2_chunk_scan(q, k, v, logf, s0):
    B_, T, H, DK = q.shape
    DV = v.shape[-1]
    C = A_CHUNK
    N = T // C
    f32 = jnp.float32

    def rs(a):
        return a.astype(f32).reshape(B_, N, C, H, a.shape[-1]).transpose(0, 3, 1, 2, 4)

    qc, kc, vc, gc = rs(q), rs(k), rs(v), rs(logf)
    bc = jnp.cumsum(gc, axis=3)
    causal = jnp.tril(jnp.ones((C, C), dtype=bool))
    diff = bc[:, :, :, :, None, :] - bc[:, :, :, None, :, :]
    decay = jnp.exp(jnp.where(causal[:, :, None], diff, -jnp.inf))
    scores = jnp.einsum('bhntk,bhnsk,bhntsk->bhnts', qc, kc, decay)
    o_intra = jnp.einsum('bhnts,bhnsv->bhntv', scores, vc)
    btot = bc[:, :, :, -1, :]
    k_end = kc * jnp.exp(btot[:, :, :, None, :] - bc)
    ds = jnp.einsum('bhnsk,bhnsv->bhnkv', k_end, vc)

    def step(s, inp):
        dec, d = inp
        return dec[..., None] * s + d, s

    s_final, s_start = lax.scan(step, s0.astype(f32),
                                (jnp.moveaxis(jnp.exp(btot), 2, 0), jnp.moveaxis(ds, 2, 0)))
    s_start = jnp.moveaxis(s_start, 0, 2)
    o_inter = jnp.einsum('bhntk,bhnkv->bhntv', qc * jnp.exp(bc), s_start)
    o = (o_intra + o_inter).transpose(0, 2, 3, 1, 4).reshape(B_, T, H, DV)
    return o, s_final


def hgrn2_mixer(aq, ai, af_f, af_b, ag, lb_f, lb_b, g_norm, s0_f, s0_b):
    B_, T, _ = aq.shape
    q = jax.nn.silu(split_heads(aq, A_HEADS))
    i = split_heads(ai, A_HEADS)

    def gates(af, lb):
        lbh = lb.reshape(A_HEADS, A_DK)
        f = lbh + (1.0 - lbh) * jax.nn.sigmoid(split_heads(af, A_HEADS).astype(jnp.float32))
        return 1.0 - f, jnp.log(f)

    k_f, lf_f = gates(af_f, lb_f)
    k_b, lf_b = gates(af_b, lb_b)
    o_f, s_f = hgrn2_chunk_scan(q, k_f, i, lf_f, s0_f)
    flip = lambda a: jnp.flip(a, axis=1)
    o_b, s_b = hgrn2_chunk_scan(flip(q), flip(k_b), flip(i), flip(lf_b), s0_b)
    o = o_f + flip(o_b)
    o = o * lax.rsqrt(jnp.mean(o * o, axis=-1, keepdims=True) + EPS)
    o = o * g_norm.astype(jnp.float32).reshape(A_HEADS, A_DV)
    o = o.reshape(B_, T, A_WIDTH).astype(ai.dtype) * jax.nn.silu(ag)
    return o, s_f, s_b


def context_attention(q, k, v, sink):
    B_, L, H, hd = q.shape
    KV = k.shape[2]
    G = H // KV
    nq = L // B_BLOCK
    scale = hd ** -0.5
    qb = q.reshape(B_, nq, B_BLOCK, KV, G, hd).transpose(1, 0, 2, 3, 4, 5)
    sink_col = sink.astype(jnp.float32).reshape(KV, G)

    def blk(qi):
        s = jnp.einsum('bqkgd,bskd->bkgqs', qi, k).astype(jnp.float32) * scale
        sc = jnp.broadcast_to(sink_col[None, :, :, None, None], s.shape[:-1] + (1,))
        p = jax.nn.softmax(jnp.concatenate([s, sc], axis=-1), axis=-1)[..., :-1]
        return jnp.einsum('bkgqs,bskd->bqkgd', p.astype(v.dtype), v)

    o = lax.map(blk, qb)
    return o.transpose(1, 0, 2, 3, 4, 5).reshape(B_, L, H * hd)


def latent_window_attention(q, k, v, ck, cv, sink):
    B_, T, H, hd = q.shape
    KV = k.shape[2]
    G = H // KV
    nb = T // B_BLOCK
    Lc = ck.shape[1]
    scale = hd ** -0.5
    qb = q.reshape(B_, nb, B_BLOCK, KV, G, hd)

    def band(a):
        ap = jnp.pad(a, ((0, 0), (B_BLOCK, B_BLOCK), (0, 0), (0, 0))).reshape(B_, nb + 2, B_BLOCK, KV, hd)
        return jnp.concatenate([ap[:, :-2], ap[:, 1:-1], ap[:, 2:]], axis=2)

    kw, vw = band(k), band(v)
    qpos = jnp.arange(nb)[:, None] * B_BLOCK + jnp.arange(B_BLOCK)[None, :]
    kpos = jnp.arange(nb)[:, None] * B_BLOCK + jnp.arange(3 * B_BLOCK)[None, :] - B_BLOCK
    kp = kpos[:, None, :]
    mask = (jnp.abs(kp - qpos[:, :, None]) <= WINDOW) & (kp >= 0) & (kp < T)
    s_loc = jnp.einsum('bnqkgd,bnskd->bkgnqs', qb, kw).astype(jnp.float32) * scale
    s_loc = jnp.where(mask, s_loc, -jnp.inf)
    s_ctx = jnp.einsum('bnqkgd,bskd->bkgnqs', qb, ck).astype(jnp.float32) * scale
    sc = jnp.broadcast_to(sink.astype(jnp.float32).reshape(KV, G)[None, :, :, None, None, None],
                          s_loc.shape[:-1] + (1,))
    p = jax.nn.softmax(jnp.concatenate([s_loc, s_ctx, sc], axis=-1), axis=-1).astype(v.dtype)
    o = (jnp.einsum('bkgnqs,bnskd->bnqkgd', p[..., :3 * B_BLOCK], vw)
         + jnp.einsum('bkgnqs,bskd->bnqkgd', p[..., 3 * B_BLOCK:3 * B_BLOCK + Lc], cv))
    return o.reshape(B_, T, H * hd)


def peer(h, w_q, sub_keys, w_u, w_v):
    B_, T, D = h.shape
    xs = h.reshape(-1, P_TOK_BLOCK, D)

    def per_block(xb):
        n = xb.shape[0]
        q = (xb @ w_q).reshape(n, P_HEADS, 2, P_QDIM // 2)
        s = jnp.einsum('nhpd,hpkd->nhpk', q, sub_keys).astype(jnp.float32)
        s1, i1 = lax.top_k(s[:, :, 0], P_TOPK)
        s2, i2 = lax.top_k(s[:, :, 1], P_TOPK)
        cand_s = (s1[..., :, None] + s2[..., None, :]).reshape(n, P_HEADS, P_TOPK * P_TOPK)
        cand_i = (i1[..., :, None] * P_NKEYS + i2[..., None, :]).reshape(n, P_HEADS, P_TOPK * P_TOPK)
        top_s, pos = lax.top_k(cand_s, P_TOPK)
        idx = jnp.take_along_axis(cand_i, pos, axis=-1)
        g = jax.nn.softmax(top_s, axis=-1)
        u = jnp.take(w_u, idx, axis=0)
        act = jax.nn.gelu(jnp.einsum('nhkd,nd->nhk', u, xb).astype(jnp.float32), approximate=False)
        v = jnp.take(w_v, idx, axis=0)
        return jnp.einsum('nhk,nhkd->nd', (g * act).astype(xb.dtype), v)

    return lax.map(per_block, xs).reshape(B_, T, D)


def trunk_layer(x, cond, w_ada, b_ada, g1, g2, w_in, lb_f, lb_b, a_norm, sink, w_oa, w_ob, w_out,
                p_wq, p_keys, p_u, p_v, s0_f, s0_b, ctx_k, ctx_v):
    sh1, sc1, gt1, sh2, sc2, gt2 = adaln(cond, w_ada, b_ada)
    h = rmsnorm(x, g1) * (1.0 + sc1) + sh1
    aq, ai, aff, afb, ag, bq, bk, bv, ga, gb = split_columns(h @ w_in)
    o_a, s_f, s_b = hgrn2_mixer(aq, ai, aff, afb, ag, lb_f, lb_b, a_norm, s0_f, s0_b)
    q = split_heads(bq, B_HEADS)
    k = split_heads(bk, B_KV_HEADS)
    v = split_heads(bv, B_KV_HEADS)
    if ctx_k is None:
        o_b = context_attention(q, k, v, sink)
    else:
        o_b = latent_window_attention(axial_rope(q), axial_rope(k), v, ctx_k, ctx_v, sink)
    y = (jax.nn.sigmoid(ga) * (o_a @ w_oa) + jax.nn.sigmoid(gb) * (o_b @ w_ob)) @ w_out
    x = x + gt1 * y
    h2 = rmsnorm(x, g2) * (1.0 + sc2) + sh2
    x = x + gt2 * peer(h2, p_wq, p_keys, p_u, p_v)
    return x, s_f, s_b, k, v


def setup_inputs(seed: int = 0) -> dict:
    key = jax.random.key(seed)
    ks = jax.random.split(key, 24)
    f32 = jnp.float32
    nrm = lambda k, shape, s: jax.random.normal(k, shape, f32) * s
    return {
        'x_prompt': nrm(ks[0], (BATCH, SEQ, D_MODEL), 1.0),
        'x_sample': nrm(ks[1], (DEC_BATCH, DEC_SEQ, D_MODEL), 1.0),
        'state_hgrn': nrm(ks[2], (DEC_BATCH, DEPTH, 2, A_HEADS, A_DK, A_DV), 0.5),
        'cache_k': nrm(ks[3], (DEC_BATCH, DEPTH, PAST_LEN, B_KV_HEADS, B_HEAD_DIM), 1.0),
        'cache_v': nrm(ks[4], (DEC_BATCH, DEPTH, PAST_LEN, B_KV_HEADS, B_HEAD_DIM), 1.0),
        'c': nrm(ks[5], (DEC_BATCH, D_MODEL), 1.0),
        'c_ctx': nrm(ks[6], (D_MODEL,), 1.0),
        'w_ada': nrm(ks[7], (DEPTH, D_MODEL, 6 * D_MODEL), 0.5 * D_MODEL ** -0.5),
        'b_ada': nrm(ks[8], (DEPTH, 6 * D_MODEL), 0.02),
        'norm1': 1.0 + nrm(ks[9], (DEPTH, D_MODEL), 0.02),
        'norm2': 1.0 + nrm(ks[10], (DEPTH, D_MODEL), 0.02),
        'w_in': nrm(ks[11], (DEPTH, D_MODEL, D_IN), D_MODEL ** -0.5),
        'lb_logits': nrm(ks[12], (2, DEPTH + 1, A_HEADS * A_DK), 0.1),
        'a_norm': 1.0 + nrm(ks[13], (DEPTH, A_WIDTH), 0.02),
        'b_sink': nrm(ks[14], (DEPTH, B_HEADS), 0.5),
        'w_oa': nrm(ks[15], (DEPTH, A_WIDTH, D_MODEL), A_WIDTH ** -0.5),
        'w_ob': nrm(ks[16], (DEPTH, B_WIDTH, D_MODEL), B_WIDTH ** -0.5),
        'w_out': nrm(ks[17], (DEPTH, D_MODEL, D_MODEL), D_MODEL ** -0.5),
        'p_wq': nrm(ks[18], (DEPTH, D_MODEL, P_HEADS * P_QDIM), D_MODEL ** -0.5),
        'p_keys': nrm(ks[19], (DEPTH, P_HEADS, 2, P_NKEYS, P_QDIM // 2), (P_QDIM // 2) ** -0.5),
        'p_u': nrm(ks[20], (DEPTH, P_EXPERTS, D_MODEL), D_MODEL ** -0.5),
        'p_v': nrm(ks[21], (DEPTH, P_EXPERTS, D_MODEL), P_HEADS ** -0.5),
        'norm_f': 1.0 + nrm(ks[22], (D_MODEL,), 0.02),
    }


def reference(x_prompt, x_sample, state_hgrn, cache_k, cache_v, c, c_ctx, w_ada, b_ada, norm1, norm2,
              w_in, lb_logits, a_norm, b_sink, w_oa, w_ob, w_out, p_wq, p_keys, p_u, p_v, norm_f):
    f32 = jnp.float32
    lb = jnp.cumsum(jax.nn.softmax(lb_logits.astype(f32), axis=1), axis=1)
    bp = x_prompt.shape[0]
    zeros = jnp.zeros((bp, A_HEADS, A_DK, A_DV), f32)
    xp, xs = x_prompt, x_sample
    states, kcs, vcs = [], [], []
    for l in range(DEPTH):
        lw = (w_ada[l], b_ada[l], norm1[l], norm2[l], w_in[l], lb[0, l], lb[1, l], a_norm[l], b_sink[l],
              w_oa[l], w_ob[l], w_out[l], p_wq[l], p_keys[l], p_u[l], p_v[l])
        xp, sf, sb, kc, vc = trunk_layer(xp, c_ctx[None, :], *lw, zeros, zeros, None, None)
        xs, _, _, _, _ = trunk_layer(xs, c, *lw, state_hgrn[:, l, 0].astype(f32),
                                     state_hgrn[:, l, 1].astype(f32), cache_k[:, l], cache_v[:, l])
        states.append(jnp.stack([sf, sb], axis=1).astype(x_prompt.dtype))
        kcs.append(kc)
        vcs.append(vc)
    y_prompt = rmsnorm(xp, norm_f)
    y_sample = rmsnorm(xs, norm_f)
    new_state_hgrn = jnp.stack(states, axis=1)
    new_cache_k = jnp.stack(kcs, axis=1)
    new_cache_v = jnp.stack(vcs, axis=1)
    return (y_prompt, y_sample, new_state_hgrn, new_cache_k, new_cache_v)
```

```python
import functools

import jax
import jax.numpy as jnp
import numpy as np
from jax import lax
from jax.experimental import pallas as pl
from jax.experimental.pallas import tpu as pltpu

F32 = jnp.float32
BF16 = jnp.bfloat16
HIGHEST = lax.Precision.HIGHEST

D_MODEL = 1024
GRID_W = 64
EPS = 1e-6
A_HEADS = 4
A_DK = 128
A_DV = 128
A_WIDTH = A_HEADS * A_DV
A_CHUNK = 16
B_HEADS = 8
B_KV_HEADS = 2
B_HEAD_DIM = 64
B_WIDTH = B_HEADS * B_HEAD_DIM
WINDOW = 128
B_BLOCK = 128
ROPE_BASE = 10000.0
P_HEADS = 8
P_NKEYS = 128
P_QDIM = 256
P_TOPK = 16

PA_W = 5 * A_WIDTH
PB_W = B_WIDTH + 2 * B_KV_HEADS * B_HEAD_DIM
PG_W = 2 * D_MODEL

LANES = 128
SUBLANES = 8
VMEM_LIMIT = 56 * 1024 * 1024

PEER_TM = 256
PEER_E1 = 2
G_PITCH = P_NKEYS + SUBLANES


def _cparams(sem):
    return pltpu.CompilerParams(dimension_semantics=sem, vmem_limit_bytes=VMEM_LIMIT)


def _sigmoid(x):
    return jax.nn.sigmoid(x)


def _silu(x):
    return x * jax.nn.sigmoid(x)


def _rmsnorm(x, g):
    return x * lax.rsqrt(jnp.mean(x * x, axis=-1, keepdims=True) + EPS) * g


def _dot_nt(a, b, **kw):
    return lax.dot_general(a, b, (((1,), (1,)), ((), ())), preferred_element_type=F32, **kw)


def _ada_kernel(c_ref, w_ref, b_ref, o_ref):
    s = _silu(c_ref[...])
    o_ref[...] = jnp.dot(s, w_ref[...], precision=HIGHEST, preferred_element_type=F32) + b_ref[...]


def _ada(cond, w, b):
    n = cond.shape[0]
    nc = w.shape[1] // D_MODEL
    return pl.pallas_call(
        _ada_kernel,
        grid=(nc,),
        in_specs=[pl.BlockSpec((n, D_MODEL), lambda j: (0, 0)),
                  pl.BlockSpec((D_MODEL, D_MODEL), lambda j: (0, j)),
                  pl.BlockSpec((1, D_MODEL), lambda j: (0, j))],
        out_specs=pl.BlockSpec((n, D_MODEL), lambda j: (0, j)),
        out_shape=jax.ShapeDtypeStruct((n, w.shape[1]), F32),
        compiler_params=_cparams(("arbitrary",)),
        name="ada",
    )(cond, w, b)


def _mod_spec(chunk, row_fn, tm):
    return pl.BlockSpec((None, 1, D_MODEL), lambda i, *_: (row_fn(i * tm), 0, chunk))


def _inproj_kernel(x_ref, g_ref, sh_ref, sc_ref, w_ref, pa_ref, pb_ref, pg_ref):
    h = _rmsnorm(x_ref[...], g_ref[...]) * (1.0 + sc_ref[...]) + sh_ref[...]
    hb = h.astype(BF16)
    pa_ref[...] = jnp.dot(hb, w_ref[:, :PA_W], preferred_element_type=F32)
    pb_ref[...] = jnp.dot(hb, w_ref[:, PA_W:PA_W + PB_W], preferred_element_type=F32)
    pg_ref[...] = jnp.dot(hb, w_ref[:, PA_W + PB_W:], preferred_element_type=F32)


def _inproj(x2, g1, mod3, w_in_bf, row_fn, tm=256):
    t = x2.shape[0]
    d_in = w_in_bf.shape[1]
    return pl.pallas_call(
        _inproj_kernel,
        grid=(t // tm,),
        in_specs=[pl.BlockSpec((tm, D_MODEL), lambda i: (i, 0)),
                  pl.BlockSpec((1, D_MODEL), lambda i: (0, 0)),
                  _mod_spec(0, row_fn, tm), _mod_spec(1, row_fn, tm),
                  pl.BlockSpec((D_MODEL, d_in), lambda i: (0, 0))],
        out_specs=[pl.BlockSpec((tm, PA_W), lambda i: (i, 0)),
                   pl.BlockSpec((tm, PB_W), lambda i: (i, 0)),
                   pl.BlockSpec((tm, PG_W), lambda i: (i, 0))],
        out_shape=[jax.ShapeDtypeStruct((t, PA_W), F32),
                   jax.ShapeDtypeStruct((t, PB_W), F32),
                   jax.ShapeDtypeStruct((t, PG_W), F32)],
        compiler_params=_cparams(("arbitrary",)),
        name="inproj",
    )(x2, g1, mod3, mod3, w_in_bf)


def _hgrn_intra(bc, q, k, v, reverse):
    c = A_CHUNK
    half = SUBLANES
    row = lax.broadcasted_iota(jnp.int32, (c, 1), 0)
    acc = [jnp.zeros((half, A_DV), F32), jnp.zeros((half, A_DV), F32)]
    for s in range(c):
        if reverse:
            parts = [0] if s < half else [0, 1]
        else:
            parts = [0, 1] if s < half else [1]
        bcs = bc[s:s + 1, :]
        ks = k[s:s + 1, :]
        vs = v[s:s + 1, :]
        for p in parts:
            sl = slice(p * half, (p + 1) * half)
            keep = (row[sl] <= s) if reverse else (row[sl] >= s)
            d = jnp.where(keep, bc[sl] - bcs, -jnp.inf)
            w = jnp.exp(d) * (q[sl] * ks)
            sc = jnp.sum(w, axis=-1, keepdims=True)
            acc[p] = acc[p] + sc * vs
    return jnp.concatenate(acc, axis=0)


def _hgrn_block(blk, reverse, st, aq_ref, ai_ref, af_ref, lb, cum_m, rest_m):
    bt = LANES
    rows = pl.ds(pl.multiple_of(blk * bt, bt), bt)
    q = _silu(aq_ref[rows, :])
    v = ai_ref[rows, :]
    f = lb + (1.0 - lb) * _sigmoid(af_ref[rows, :])
    k = 1.0 - f
    lf = jnp.log(f)
    bc = jnp.dot(cum_m, lf, precision=HIGHEST, preferred_element_type=F32)
    rest = jnp.dot(rest_m, lf, precision=HIGHEST, preferred_element_type=F32)
    qe = (q * jnp.exp(bc)).astype(BF16)
    ke = (k * jnp.exp(rest)).astype(BF16)
    ebt = jnp.exp(bc + rest)
    vt = v.T.astype(BF16)
    tok = lax.broadcasted_iota(jnp.int32, (1, bt), 1) // A_CHUNK
    nchunk = bt // A_CHUNK
    outs = [None] * nchunk
    order = range(nchunk - 1, -1, -1) if reverse else range(nchunk)
    for c in order:
        cs = slice(c * A_CHUNK, (c + 1) * A_CHUNK)
        o_inter = _dot_nt(qe[cs], st.astype(BF16))
        o_intra = _hgrn_intra(bc[cs], q[cs], k[cs], v[cs], reverse)
        outs[c] = o_inter + o_intra
        vtm = jnp.where(tok == c, vt, jnp.zeros_like(vt))
        ds_t = jnp.dot(vtm, ke, preferred_element_type=F32)
        st = st * ebt[c * A_CHUNK:c * A_CHUNK + 1, :] + ds_t
    return jnp.concatenate(outs, axis=0), st


def _hgrn_kernel(*refs, seq, has_s0, emit_state):
    aq_ref, ai_ref, aff_ref, afb_ref, ag_ref, lbl_ref, gn_ref = refs[:7]
    pos = 7
    s0_ref = None
    if has_s0:
        s0_ref = refs[pos]
        pos += 1
    o_ref = refs[pos]
    pos += 1
    st_ref = None
    if emit_state:
        st_ref = refs[pos]
        pos += 1
    of_scr, ob_scr = refs[pos], refs[pos + 1]

    bt = LANES
    nb = seq // bt
    lbl = lbl_ref[...]

    def lower_bound(d):
        z = lbl[d]
        m = jnp.max(z, axis=0, keepdims=True)
        e = jnp.exp(z - m)
        return e[0:1, :] / jnp.sum(e, axis=0, keepdims=True)

    lb_f = lower_bound(0)
    lb_b = lower_bound(1)

    r = lax.broadcasted_iota(jnp.int32, (bt, bt), 0)
    cidx = lax.broadcasted_iota(jnp.int32, (bt, bt), 1)
    same = (r // A_CHUNK) == (cidx // A_CHUNK)
    one = jnp.ones((bt, bt), F32)
    zero = jnp.zeros((bt, bt), F32)
    cum_f = jnp.where(same & (cidx <= r), one, zero)
    rest_f = jnp.where(same & (cidx > r), one, zero)
    cum_b = jnp.where(same & (cidx >= r), one, zero)
    rest_b = jnp.where(same & (cidx < r), one, zero)

    if has_s0:
        st_f0 = s0_ref[0].T
        st_b0 = s0_ref[1].T
    else:
        st_f0 = jnp.zeros((A_DV, A_DK), F32)
        st_b0 = jnp.zeros((A_DV, A_DK), F32)

    def body(j, carry):
        st_f, st_b = carry
        o_f, st_f = _hgrn_block(j, False, st_f, aq_ref, ai_ref, aff_ref, lb_f, cum_f, rest_f)
        of_scr[pl.ds(pl.multiple_of(j * bt, bt), bt), :] = o_f
        jb = nb - 1 - j
        o_b, st_b = _hgrn_block(jb, True, st_b, aq_ref, ai_ref, afb_ref, lb_b, cum_b, rest_b)
        ob_scr[pl.ds(pl.multiple_of(jb * bt, bt), bt), :] = o_b
        return st_f, st_b

    st_f, st_b = lax.fori_loop(0, nb, body, (st_f0, st_b0))

    o = of_scr[...] + ob_scr[...]
    o = o * lax.rsqrt(jnp.mean(o * o, axis=-1, keepdims=True) + EPS) * gn_ref[...]
    o_ref[...] = (o * _silu(ag_ref[...])).astype(o_ref.dtype)
    if emit_state:
        st_ref[0] = st_f.T
        st_ref[1] = st_b.T


def _hgrn(pa3, lb_logits4, a_norm3, s0, emit_state):
    b, t, _ = pa3.shape
    has_s0 = s0 is not None

    def seg(sidx):
        return pl.BlockSpec((None, t, A_DK), lambda bi, h: (bi, 0, sidx * A_HEADS + h))

    in_specs = [seg(0), seg(1), seg(2), seg(3), seg(4),
                pl.BlockSpec((2, None, lb_logits4.shape[2], A_DK), lambda bi, h: (0, h, 0, 0)),
                pl.BlockSpec((None, 1, A_DV), lambda bi, h: (h, 0, 0))]
    args = [pa3, pa3, pa3, pa3, pa3, lb_logits4, a_norm3]
    if has_s0:
        in_specs.append(pl.BlockSpec((None, None, 2, None, A_DK, A_DV), lambda bi, h: (bi, 0, 0, h, 0, 0)))
        args.append(s0)
    out_specs = [pl.BlockSpec((None, t, A_DV), lambda bi, h: (bi, 0, h))]
    out_shape = [jax.ShapeDtypeStruct((b, t, A_WIDTH), BF16)]
    if emit_state:
        out_specs.append(pl.BlockSpec((None, None, 2, None, A_DK, A_DV), lambda bi, h: (bi, 0, 0, h, 0, 0)))
        out_shape.append(jax.ShapeDtypeStruct((b, 1, 2, A_HEADS, A_DK, A_DV), F32))
    return pl.pallas_call(
        functools.partial(_hgrn_kernel, seq=t, has_s0=has_s0, emit_state=emit_state),
        grid=(b, A_HEADS),
        in_specs=in_specs,
        out_specs=out_specs,
        out_shape=out_shape,
        scratch_shapes=[pltpu.VMEM((t, A_DV), F32), pltpu.VMEM((t, A_DV), F32)],
        compiler_params=_cparams(("arbitrary", "arbitrary")),
        name="hgrn_state" if emit_state else "hgrn",
    )(*args)


def _dup_half(x, kv, lane):
    swapped = pltpu.roll(x, B_HEAD_DIM, axis=1)
    first = lane < B_HEAD_DIM
    return jnp.where(first, x, swapped) if kv == 0 else jnp.where(first, swapped, x)


def _attend_heads(q_of_pair, key_sets, sink_ref, o_ref, lane):
    scale = B_HEAD_DIM ** -0.5
    group = B_HEADS // B_KV_HEADS
    first = lane < B_HEAD_DIM
    for kv in range(B_KV_HEADS):
        kvs = [(_dup_half(k, kv, lane).astype(BF16), _dup_half(v, kv, lane).astype(BF16), m)
               for k, v, m in key_sets]
        for jj in range(group // 2):
            j = kv * (group // 2) + jj
            qp = q_of_pair(j)
            halves = []
            for half in range(2):
                h = 2 * j + half
                qm = jnp.where(first if half == 0 else jnp.logical_not(first), qp, jnp.zeros_like(qp))
                qm = qm.astype(BF16)
                sink = sink_ref[h]
                ss = []
                for k2, _, m in kvs:
                    s = _dot_nt(qm, k2) * scale
                    if m is not None:
                        s = jnp.where(m, s, -jnp.inf)
                    ss.append(s)
                mx = sink
                for s in ss:
                    mx = jnp.maximum(mx, jnp.max(s, axis=-1, keepdims=True))
                es = [jnp.exp(s - mx) for s in ss]
                den = jnp.exp(sink - mx)
                for e in es:
                    den = den + jnp.sum(e, axis=-1, keepdims=True)
                inv = 1.0 / den
                oh = None
                for e, (_, v2, _) in zip(es, kvs):
                    part = jnp.dot((e * inv).astype(BF16), v2, preferred_element_type=F32)
                    oh = part if oh is None else oh + part
                halves.append(oh)
            o_ref[:, j * LANES:(j + 1) * LANES] = jnp.where(first, halves[0], halves[1]).astype(o_ref.dtype)


def _ctx_attn_kernel(sink_ref, q_ref, kv_ref, o_ref):
    lane = lax.broadcasted_iota(jnp.int32, (1, LANES), 1)
    kvv = kv_ref[...]
    key_sets = [(kvv[:, :LANES], kvv[:, LANES:], None)]
    _attend_heads(lambda j: q_ref[:, j * LANES:(j + 1) * LANES], key_sets, sink_ref, o_ref, lane)


def _ctx_attn(pb3, sink):
    b, t, _ = pb3.shape
    return pl.pallas_call(
        _ctx_attn_kernel,
        grid=(b,),
        in_specs=[pl.BlockSpec(memory_space=pltpu.SMEM),
                  pl.BlockSpec((None, t, B_WIDTH), lambda bi: (bi, 0, 0)),
                  pl.BlockSpec((None, t, 2 * LANES), lambda bi: (bi, 0, B_WIDTH // (2 * LANES)))],
        out_specs=pl.BlockSpec((None, t, B_WIDTH), lambda bi: (bi, 0, 0)),
        out_shape=jax.ShapeDtypeStruct((b, t, B_WIDTH), BF16),
        compiler_params=_cparams(("arbitrary",)),
        name="ctx_attn",
    )(sink, pb3, pb3)


def _rope(x, cos, sin_signed, lane):
    nf = B_HEAD_DIM // 4
    up = pltpu.roll(x, LANES - nf, axis=1)
    down = pltpu.roll(x, nf, axis=1)
    partner = jnp.where((lane % (2 * nf)) < nf, up, down)
    return x * cos + partner * sin_signed


def _lat_attn_kernel(sink_ref, q_ref, kvl_ref, kvc_ref, kvr_ref, ck_ref, cv_ref,
                     cl_ref, cc_ref, cr_ref, sl_ref, sc_ref, sr_ref, o_ref, *, seq):
    n = pl.program_id(1)
    blk = B_BLOCK
    lane = lax.broadcasted_iota(jnp.int32, (1, LANES), 1)
    kw = jnp.concatenate([kvl_ref[:, :LANES], kvc_ref[:, :LANES], kvr_ref[:, :LANES]], axis=0)
    vw = jnp.concatenate([kvl_ref[:, LANES:], kvc_ref[:, LANES:], kvr_ref[:, LANES:]], axis=0)
    cos_w = jnp.concatenate([cl_ref[...], cc_ref[...], cr_ref[...]], axis=0)
    sin_w = jnp.concatenate([sl_ref[...], sc_ref[...], sr_ref[...]], axis=0)
    kw = _rope(kw, cos_w, sin_w, lane)
    qpos = n * blk + lax.broadcasted_iota(jnp.int32, (blk, 3 * blk), 0)
    kpos = (n - 1) * blk + lax.broadcasted_iota(jnp.int32, (blk, 3 * blk), 1)
    mask = (jnp.abs(kpos - qpos) <= WINDOW) & (kpos >= 0) & (kpos < seq)
    key_sets = [(kw, vw, mask), (ck_ref[...], cv_ref[...], None)]
    cos_c = cc_ref[...]
    sin_c = sc_ref[...]

    def q_of_pair(j):
        return _rope(q_ref[:, j * LANES:(j + 1) * LANES], cos_c, sin_c, lane)

    _attend_heads(q_of_pair, key_sets, sink_ref, o_ref, lane)


def _lat_attn(pb3, ck, cv, cos_t, sin_t, sink):
    b, t, _ = pb3.shape
    nb = t // B_BLOCK
    lc = ck.shape[1]
    kv_col = B_WIDTH // (2 * LANES)

    def kv_spec(off):
        return pl.BlockSpec((None, B_BLOCK, 2 * LANES),
                            lambda bi, n: (bi, jnp.clip(n + off, 0, nb - 1), kv_col))

    def tab_spec(off):
        return pl.BlockSpec((B_BLOCK, LANES), lambda bi, n: (jnp.clip(n + off, 0, nb - 1), 0))

    return pl.pallas_call(
        functools.partial(_lat_attn_kernel, seq=t),
        grid=(b, nb),
        in_specs=[pl.BlockSpec(memory_space=pltpu.SMEM),
                  pl.BlockSpec((None, B_BLOCK, B_WIDTH), lambda bi, n: (bi, n, 0)),
                  kv_spec(-1), kv_spec(0), kv_spec(1),
                  pl.BlockSpec((None, lc, LANES), lambda bi, n: (bi, 0, 0)),
                  pl.BlockSpec((None, lc, LANES), lambda bi, n: (bi, 0, 0)),
                  tab_spec(-1), tab_spec(0), tab_spec(1),
                  tab_spec(-1), tab_spec(0), tab_spec(1)],
        out_specs=pl.BlockSpec((None, B_BLOCK, B_WIDTH), lambda bi, n: (bi, n, 0)),
        out_shape=jax.ShapeDtypeStruct((b, t, B_WIDTH), BF16),
        compiler_params=_cparams(("arbitrary", "arbitrary")),
        name="lat_attn",
    )(sink, pb3, pb3, pb3, pb3, ck, cv, cos_t, cos_t, cos_t, sin_t, sin_t, sin_t)


def _rope_tables(seq):
    half = B_HEAD_DIM // 2
    nf = half // 2
    row = jnp.repeat(jnp.arange(seq // GRID_W), GRID_W).astype(F32)
    col = jnp.tile(jnp.arange(GRID_W), seq // GRID_W).astype(F32)
    inv = ROPE_BASE ** (-jnp.arange(nf, dtype=F32) / nf)
    ang_r = row[:, None] * inv[None, :]
    ang_c = col[:, None] * inv[None, :]

    def part(ang):
        c = jnp.cos(ang)
        s = jnp.sin(ang)
        return jnp.concatenate([c, c], axis=-1), jnp.concatenate([-s, s], axis=-1)

    cr, sr = part(ang_r)
    cc, sc = part(ang_c)
    cos_h = jnp.concatenate([cr, cc], axis=-1)
    sin_h = jnp.concatenate([sr, sc], axis=-1)
    reps = LANES // B_HEAD_DIM
    return jnp.tile(cos_h, (1, reps)), jnp.tile(sin_h, (1, reps))


def _postmix_kernel(oa_ref, ob_ref, ga_ref, gb_ref, x_ref, gt1_ref, sh2_ref, sc2_ref, g2_ref,
                    woa_ref, wob_ref, wout_ref, wq_ref, keys_ref, x1_ref, h2_ref, s_ref):
    ya = jnp.dot(oa_ref[...], woa_ref[...], preferred_element_type=F32)
    yb = jnp.dot(ob_ref[...], wob_ref[...], preferred_element_type=F32)
    u = _sigmoid(ga_ref[...]) * ya + _sigmoid(gb_ref[...]) * yb
    y = jnp.dot(u.astype(BF16), wout_ref[...], preferred_element_type=F32)
    x1 = x_ref[...] + gt1_ref[...] * y
    x1_ref[...] = x1
    h2 = _rmsnorm(x1, g2_ref[...]) * (1.0 + sc2_ref[...]) + sh2_ref[...]
    h2b = h2.astype(BF16)
    h2_ref[...] = h2b
    q = jnp.dot(h2b, wq_ref[...], preferred_element_type=F32)
    for hp in range(2 * P_HEADS):
        cs = slice(hp * LANES, (hp + 1) * LANES)
        s_ref[:, cs] = _dot_nt(q[:, cs].astype(BF16), keys_ref[hp])


def _postmix(oa, ob, pg, x2, mod3, g2, woa, wob, wout, wq, keys, row_fn, tm=256):
    t = x2.shape[0]
    nq = wq.shape[1]
    full = lambda shape: pl.BlockSpec(shape, lambda i: (0,) * len(shape))
    return pl.pallas_call(
        _postmix_kernel,
        grid=(t // tm,),
        in_specs=[pl.BlockSpec((tm, A_WIDTH), lambda i: (i, 0)),
                  pl.BlockSpec((tm, B_WIDTH), lambda i: (i, 0)),
                  pl.BlockSpec((tm, D_MODEL), lambda i: (i, 0)),
                  pl.BlockSpec((tm, D_MODEL), lambda i: (i, 1)),
                  pl.BlockSpec((tm, D_MODEL), lambda i: (i, 0)),
                  _mod_spec(2, row_fn, tm), _mod_spec(3, row_fn, tm), _mod_spec(4, row_fn, tm),
                  full((1, D_MODEL)),
                  full(woa.shape), full(wob.shape), full(wout.shape), full(wq.shape), full(keys.shape)],
        out_specs=[pl.BlockSpec((tm, D_MODEL), lambda i: (i, 0)),
                   pl.BlockSpec((tm, D_MODEL), lambda i: (i, 0)),
                   pl.BlockSpec((tm, nq), lambda i: (i, 0))],
        out_shape=[jax.ShapeDtypeStruct((t, D_MODEL), F32),
                   jax.ShapeDtypeStruct((t, D_MODEL), BF16),
                   jax.ShapeDtypeStruct((t, nq), F32)],
        compiler_params=_cparams(("arbitrary",)),
        name="postmix",
    )(oa, ob, pg, pg, x2, mod3, mod3, mod3, g2, woa, wob, wout, wq, keys)


def _topk_kernel(s_ref, e1_ref, e2_ref, g_ref):
    tt = s_ref.shape[0]
    nk = P_NKEYS
    kk = P_TOPK
    ncand = kk * kk
    lane_k = lax.broadcasted_iota(jnp.int32, (tt, nk), 1)
    lane_c = lax.broadcasted_iota(jnp.int32, (tt, ncand), 1)
    lane_o = lax.broadcasted_iota(jnp.int32, (tt, P_HEADS * kk), 1)
    neg = -jnp.inf
    top_s = jnp.zeros((tt, P_HEADS * kk), F32)
    top_e = jnp.zeros((tt, P_HEADS * kk), F32)
    top_m = jnp.zeros((tt, P_HEADS * kk), F32)
    for h in range(P_HEADS):
        reps = []
        for side in range(2):
            s = s_ref[:, (2 * h + side) * nk:(2 * h + side + 1) * nk]
            vrep = jnp.zeros((tt, ncand), F32)
            irep = jnp.zeros((tt, ncand), F32)
            for a in range(kk):
                m = jnp.max(s, axis=-1, keepdims=True)
                i = jnp.argmax(s, axis=-1, keepdims=True).astype(jnp.int32)
                s = jnp.where(lane_k == i, neg, s)
                sel = ((lane_c // kk) == a) if side == 0 else ((lane_c % kk) == a)
                vrep = jnp.where(sel, m, vrep)
                irep = jnp.where(sel, i.astype(F32), irep)
            reps.append((vrep, irep))
        cand = reps[0][0] + reps[1][0]
        eid = reps[0][1] * float(nk) + reps[1][1]
        for k in range(kk):
            m = jnp.max(cand, axis=-1, keepdims=True)
            pos = jnp.argmax(cand, axis=-1, keepdims=True).astype(jnp.int32)
            hit = lane_c == pos
            e = jnp.max(jnp.where(hit, eid, -1.0), axis=-1, keepdims=True)
            cand = jnp.where(hit, neg, cand)
            col = lane_o == (h * kk + k)
            top_s = jnp.where(col, m, top_s)
            top_e = jnp.where(col, e, top_e)
            if k == 0:
                top_m = jnp.where((lane_o // kk) == h, m, top_m)
    p = jnp.exp(top_s - top_m)
    r = lax.broadcasted_iota(jnp.int32, (P_HEADS * kk, P_HEADS * kk), 0)
    c = lax.broadcasted_iota(jnp.int32, (P_HEADS * kk, P_HEADS * kk), 1)
    seg = jnp.where((r // kk) == (c // kk), 1.0, 0.0).astype(F32)
    den = jnp.dot(p, seg, precision=HIGHEST, preferred_element_type=F32)
    g_ref[...] = p / den
    ei = top_e.astype(jnp.int32)
    e1_ref[...] = ei // nk
    e2_ref[...] = ei % nk


def _topk(scores, tt=64):
    t = scores.shape[0]
    w = P_HEADS * P_TOPK
    return pl.pallas_call(
        _topk_kernel,
        grid=(t // tt,),
        in_specs=[pl.BlockSpec((tt, scores.shape[1]), lambda i: (i, 0))],
        out_specs=[pl.BlockSpec((tt, w), lambda i: (i, 0))] * 3,
        out_shape=[jax.ShapeDtypeStruct((t, w), jnp.int32),
                   jax.ShapeDtypeStruct((t, w), jnp.int32),
                   jax.ShapeDtypeStruct((t, w), F32)],
        compiler_params=_cparams(("arbitrary",)),
        name="topk",
    )(scores)


def _gelu(x):
    return 0.5 * x * (1.0 + lax.erf(x * (2.0 ** -0.5)))


def _peer_kernel(h2_ref, e1_ref, e2_ref, g_ref, wu_ref, wv_ref, x1_ref, gt2_ref, nf_ref, o_ref,
                 gate_scr, acc_scr):
    e = pl.program_id(1)
    tm = h2_ref.shape[0]
    nk = P_NKEYS

    @pl.when(e == 0)
    def _build_gates():
        acc_scr[...] = jnp.zeros_like(acc_scr)
        sub = lax.broadcasted_iota(jnp.int32, (nk, P_HEADS * P_TOPK), 0)

        def per_token(t, carry):
            a = e1_ref[pl.ds(t, 1), :]
            b = e2_ref[pl.ds(t, 1), :]
            w = g_ref[pl.ds(t, 1), :]
            pt = jnp.where(sub == a, w, 0.0).astype(BF16)
            qt = jnp.where(sub == b, 1.0, 0.0).astype(BF16)
            gate_scr[pl.ds(pl.multiple_of(t * G_PITCH, SUBLANES), nk), :] = _dot_nt(pt, qt)
            return carry

        lax.fori_loop(0, tm, per_token, 0)

    a = _dot_nt(h2_ref[...], wu_ref[...])
    gates = [gate_scr[pl.ds(e * PEER_E1 + r, tm, stride=G_PITCH), :] for r in range(PEER_E1)]
    hid = (jnp.concatenate(gates, axis=1) * _gelu(a)).astype(BF16)
    acc_scr[...] += jnp.dot(hid, wv_ref[...], preferred_element_type=F32)

    @pl.when(e == pl.num_programs(1) - 1)
    def _finish():
        x2 = x1_ref[...] + gt2_ref[...] * acc_scr[...]
        o_ref[...] = _rmsnorm(x2, nf_ref[...])


def _peer(h2, e1, e2, g, wu, wv, x1, mod3, norm_f, row_fn, tm=PEER_TM):
    t = h2.shape[0]
    te = PEER_E1 * P_NKEYS
    ne = wu.shape[0] // te
    return pl.pallas_call(
        _peer_kernel,
        grid=(t // tm, ne),
        in_specs=[pl.BlockSpec((tm, D_MODEL), lambda i, e: (i, 0)),
                  pl.BlockSpec((tm, LANES), lambda i, e: (i, 0)),
                  pl.BlockSpec((tm, LANES), lambda i, e: (i, 0)),
                  pl.BlockSpec((tm, LANES), lambda i, e: (i, 0)),
                  pl.BlockSpec((te, D_MODEL), lambda i, e: (e, 0)),
                  pl.BlockSpec((te, D_MODEL), lambda i, e: (e, 0)),
                  pl.BlockSpec((tm, D_MODEL), lambda i, e: (i, 0)),
                  _mod_spec(5, row_fn, tm),
                  pl.BlockSpec((1, D_MODEL), lambda i, e: (0, 0))],
        out_specs=pl.BlockSpec((tm, D_MODEL), lambda i, e: (i, 0)),
        out_shape=jax.ShapeDtypeStruct((t, D_MODEL), F32),
        scratch_shapes=[pltpu.VMEM((tm * G_PITCH, LANES), F32), pltpu.VMEM((tm, D_MODEL), F32)],
        compiler_params=_cparams(("arbitrary", "arbitrary")),
        name="peer",
    )(h2, e1, e2, g, wu, wv, x1, mod3, norm_f)


def _stream(x, mod3, row_fn, wts, s0, ctx_kv, emit_state):
    b, t, d = x.shape
    x2 = x.reshape(b * t, d)
    pa, pb, pg = _inproj(x2, wts["g1"], mod3, wts["w_in"], row_fn)
    pa3 = pa.reshape(b, t, PA_W)
    pb3 = pb.reshape(b, t, PB_W)
    res = _hgrn(pa3, wts["lb_logits"], wts["a_norm"], s0, emit_state)
    oa = res[0].reshape(b * t, A_WIDTH)
    states = res[1] if emit_state else None
    if ctx_kv is None:
        ob = _ctx_attn(pb3, wts["sink"])
    else:
        cos_t, sin_t = _rope_tables(t)
        ob = _lat_attn(pb3, ctx_kv[0], ctx_kv[1], cos_t, sin_t, wts["sink"])
    ob = ob.reshape(b * t, B_WIDTH)
    x1, h2, scores = _postmix(oa, ob, pg, x2, mod3, wts["g2"], wts["w_oa"], wts["w_ob"], wts["w_out"],
                              wts["p_wq"], wts["p_keys"], row_fn)
    e1, e2, g = _topk(scores)
    y = _peer(h2, e1, e2, g, wts["p_u"], wts["p_v"], x1, mod3, wts["norm_f"], row_fn)
    return y.reshape(b, t, d), states, pb3


def kernel(x_prompt, x_sample, state_hgrn, cache_k, cache_v, c, c_ctx, w_ada, b_ada, norm1, norm2,
           w_in, lb_logits, a_norm, b_sink, w_oa, w_ob, w_out, p_wq, p_keys, p_u, p_v, norm_f):
    depth = w_ada.shape[0]
    assert depth == 1, "single-layer step"
    bp, seq, d = x_prompt.shape
    bs, dseq, _ = x_sample.shape
    kvw = B_KV_HEADS * B_HEAD_DIM

    nrow = -(-(bs + 1) // SUBLANES) * SUBLANES
    cond = jnp.zeros((nrow, d), F32).at[:bs].set(c).at[bs].set(c_ctx)
    mod = _ada(cond, w_ada[0], b_ada[0].reshape(1, -1))
    mod3 = mod.reshape(nrow, 1, 6 * d)

    wts = dict(
        g1=norm1[0].reshape(1, d), g2=norm2[0].reshape(1, d), norm_f=norm_f.reshape(1, d),
        w_in=w_in[0].astype(BF16),
        lb_logits=lb_logits.reshape(2, depth + 1, A_HEADS, A_DK).transpose(0, 2, 1, 3),
        a_norm=a_norm[0].reshape(A_HEADS, 1, A_DV),
        sink=b_sink[0],
        w_oa=w_oa[0].astype(BF16), w_ob=w_ob[0].astype(BF16), w_out=w_out[0].astype(BF16),
        p_wq=p_wq[0].astype(BF16),
        p_keys=p_keys[0].reshape(2 * P_HEADS, P_NKEYS, P_QDIM // 2).astype(BF16),
        p_u=p_u[0].astype(BF16), p_v=p_v[0].astype(BF16),
    )

    y_prompt, states, pb3 = _stream(x_prompt, mod3, lambda tok: bs, wts, None, None, True)
    ck = cache_k[:, 0].reshape(bs, -1, kvw)
    cv = cache_v[:, 0].reshape(bs, -1, kvw)
    y_sample, _, _ = _stream(x_sample, mod3, lambda tok: tok // dseq, wts, state_hgrn, (ck, cv), False)

    new_k = pb3[:, :, B_WIDTH:B_WIDTH + kvw].reshape(bp, 1, seq, B_KV_HEADS, B_HEAD_DIM)
    new_v = pb3[:, :, B_WIDTH + kvw:].reshape(bp, 1, seq, B_KV_HEADS, B_HEAD_DIM)
    return (y_prompt, y_sample, states.astype(x_prompt.dtype), new_k, new_v)
```

```python
import functools

import jax
import jax.numpy as jnp
import numpy as np
from jax import lax
from jax.experimental import pallas as pl
from jax.experimental.pallas import tpu as pltpu

F32 = jnp.float32
BF16 = jnp.bfloat16
HIGHEST = lax.Precision.HIGHEST

D_MODEL = 1024
GRID_W = 64
EPS = 1e-6
A_HEADS = 4
A_DK = 128
A_DV = 128
A_WIDTH = A_HEADS * A_DV
A_CHUNK = 16
B_HEADS = 8
B_KV_HEADS = 2
B_HEAD_DIM = 64
B_WIDTH = B_HEADS * B_HEAD_DIM
WINDOW = 128
B_BLOCK = 128
ROPE_BASE = 10000.0
P_HEADS = 8
P_NKEYS = 128
P_QDIM = 256
P_TOPK = 16

PA_W = 5 * A_WIDTH
PB_W = B_WIDTH + 2 * B_KV_HEADS * B_HEAD_DIM
PG_W = 2 * D_MODEL

LANES = 128
SUBLANES = 8
VMEM_LIMIT = 56 * 1024 * 1024

PEER_TM = 256
PEER_E1 = 4
G_PITCH = P_NKEYS + SUBLANES


def _cparams(sem):
    return pltpu.CompilerParams(dimension_semantics=sem, vmem_limit_bytes=VMEM_LIMIT)


def _sigmoid(x):
    return jax.nn.sigmoid(x)


def _silu(x):
    return x * jax.nn.sigmoid(x)


def _rmsnorm(x, g):
    return x * lax.rsqrt(jnp.mean(x * x, axis=-1, keepdims=True) + EPS) * g


def _dot_nt(a, b, **kw):
    return lax.dot_general(a, b, (((1,), (1,)), ((), ())), preferred_element_type=F32, **kw)


def _ada_kernel(c_ref, w_ref, b_ref, o_ref):
    s = _silu(c_ref[...])
    o_ref[...] = jnp.dot(s, w_ref[...], precision=HIGHEST, preferred_element_type=F32) + b_ref[...]


def _ada(cond, w, b):
    n = cond.shape[0]
    nc = w.shape[1] // D_MODEL
    return pl.pallas_call(
        _ada_kernel,
        grid=(nc,),
        in_specs=[pl.BlockSpec((n, D_MODEL), lambda j: (0, 0)),
                  pl.BlockSpec((D_MODEL, D_MODEL), lambda j: (0, j)),
                  pl.BlockSpec((1, D_MODEL), lambda j: (0, j))],
        out_specs=pl.BlockSpec((n, D_MODEL), lambda j: (0, j)),
        out_shape=jax.ShapeDtypeStruct((n, w.shape[1]), F32),
        compiler_params=_cparams(("arbitrary",)),
        name="ada",
    )(cond, w, b)


def _mod_spec(chunk, row_fn, tm):
    return pl.BlockSpec((None, 1, D_MODEL), lambda i, *_: (row_fn(i * tm), 0, chunk))


def _inproj_kernel(x_ref, g_ref, sh_ref, sc_ref, w_ref, pa_ref, pb_ref, pg_ref):
    h = _rmsnorm(x_ref[...], g_ref[...]) * (1.0 + sc_ref[...]) + sh_ref[...]
    hb = h.astype(BF16)
    pa_ref[...] = jnp.dot(hb, w_ref[:, :PA_W], preferred_element_type=F32)
    pb_ref[...] = jnp.dot(hb, w_ref[:, PA_W:PA_W + PB_W], preferred_element_type=F32)
    pg_ref[...] = jnp.dot(hb, w_ref[:, PA_W + PB_W:], preferred_element_type=F32)


def _inproj(x2, g1, mod3, w_in_bf, row_fn, tm=256):
    t = x2.shape[0]
    d_in = w_in_bf.shape[1]
    return pl.pallas_call(
        _inproj_kernel,
        grid=(t // tm,),
        in_specs=[pl.BlockSpec((tm, D_MODEL), lambda i: (i, 0)),
                  pl.BlockSpec((1, D_MODEL), lambda i: (0, 0)),
                  _mod_spec(0, row_fn, tm), _mod_spec(1, row_fn, tm),
                  pl.BlockSpec((D_MODEL, d_in), lambda i: (0, 0))],
        out_specs=[pl.BlockSpec((tm, PA_W), lambda i: (i, 0)),
                   pl.BlockSpec((tm, PB_W), lambda i: (i, 0)),
                   pl.BlockSpec((tm, PG_W), lambda i: (i, 0))],
        out_shape=[jax.ShapeDtypeStruct((t, PA_W), F32),
                   jax.ShapeDtypeStruct((t, PB_W), F32),
                   jax.ShapeDtypeStruct((t, PG_W), F32)],
        compiler_params=_cparams(("arbitrary",)),
        name="inproj",
    )(x2, g1, mod3, mod3, w_in_bf)


def _hgrn_intra(bc, q, k, v, reverse):
    c = A_CHUNK
    half = SUBLANES
    row = lax.broadcasted_iota(jnp.int32, (c, 1), 0)
    acc = [jnp.zeros((half, A_DV), F32), jnp.zeros((half, A_DV), F32)]
    for s in range(c):
        if reverse:
            parts = [0] if s < half else [0, 1]
        else:
            parts = [0, 1] if s < half else [1]
        bcs = bc[s:s + 1, :]
        ks = k[s:s + 1, :]
        vs = v[s:s + 1, :]
        for p in parts:
            sl = slice(p * half, (p + 1) * half)
            keep = (row[sl] <= s) if reverse else (row[sl] >= s)
            d = jnp.where(keep, bc[sl] - bcs, -jnp.inf)
            w = jnp.exp(d) * (q[sl] * ks)
            sc = jnp.sum(w, axis=-1, keepdims=True)
            acc[p] = acc[p] + sc * vs
    return jnp.concatenate(acc, axis=0)


def _hgrn_block(blk, reverse, st, aq_ref, ai_ref, af_ref, lb, cum_m, rest_m):
    bt = LANES
    rows = pl.ds(pl.multiple_of(blk * bt, bt), bt)
    q = _silu(aq_ref[rows, :])
    v = ai_ref[rows, :]
    f = lb + (1.0 - lb) * _sigmoid(af_ref[rows, :])
    k = 1.0 - f
    lf = jnp.log(f)
    bc = jnp.dot(cum_m, lf, precision=HIGHEST, preferred_element_type=F32)
    rest = jnp.dot(rest_m, lf, precision=HIGHEST, preferred_element_type=F32)
    qe = (q * jnp.exp(bc)).astype(BF16)
    ke = (k * jnp.exp(rest)).astype(BF16)
    ebt = jnp.exp(bc + rest)
    vt = v.T.astype(BF16)
    tok = lax.broadcasted_iota(jnp.int32, (1, bt), 1) // A_CHUNK
    nchunk = bt // A_CHUNK
    outs = [None] * nchunk
    order = range(nchunk - 1, -1, -1) if reverse else range(nchunk)
    for c in order:
        cs = slice(c * A_CHUNK, (c + 1) * A_CHUNK)
        o_inter = _dot_nt(qe[cs], st.astype(BF16))
        o_intra = _hgrn_intra(bc[cs], q[cs], k[cs], v[cs], reverse)
        outs[c] = o_inter + o_intra
        vtm = jnp.where(tok == c, vt, jnp.zeros_like(vt))
        ds_t = jnp.dot(vtm, ke, preferred_element_type=F32)
        st = st * ebt[c * A_CHUNK:c * A_CHUNK + 1, :] + ds_t
    return jnp.concatenate(outs, axis=0), st


def _hgrn_kernel(*refs, seq, has_s0, emit_state):
    aq_ref, ai_ref, aff_ref, afb_ref, ag_ref, lbl_ref, gn_ref = refs[:7]
    pos = 7
    s0_ref = None
    if has_s0:
        s0_ref = refs[pos]
        pos += 1
    o_ref = refs[pos]
    pos += 1
    st_ref = None
    if emit_state:
        st_ref = refs[pos]
        pos += 1
    of_scr, ob_scr = refs[pos], refs[pos + 1]

    bt = LANES
    nb = seq // bt
    lbl = lbl_ref[...]

    def lower_bound(d):
        z = lbl[d]
        m = jnp.max(z, axis=0, keepdims=True)
        e = jnp.exp(z - m)
        return e[0:1, :] / jnp.sum(e, axis=0, keepdims=True)

    lb_f = lower_bound(0)
    lb_b = lower_bound(1)

    r = lax.broadcasted_iota(jnp.int32, (bt, bt), 0)
    cidx = lax.broadcasted_iota(jnp.int32, (bt, bt), 1)
    same = (r // A_CHUNK) == (cidx // A_CHUNK)
    one = jnp.ones((bt, bt), F32)
    zero = jnp.zeros((bt, bt), F32)
    cum_f = jnp.where(same & (cidx <= r), one, zero)
    rest_f = jnp.where(same & (cidx > r), one, zero)
    cum_b = jnp.where(same & (cidx >= r), one, zero)
    rest_b = jnp.where(same & (cidx < r), one, zero)

    if has_s0:
        st_f0 = s0_ref[0].T
        st_b0 = s0_ref[1].T
    else:
        st_f0 = jnp.zeros((A_DV, A_DK), F32)
        st_b0 = jnp.zeros((A_DV, A_DK), F32)

    def body(j, carry):
        st_f, st_b = carry
        o_f, st_f = _hgrn_block(j, False, st_f, aq_ref, ai_ref, aff_ref, lb_f, cum_f, rest_f)
        of_scr[pl.ds(pl.multiple_of(j * bt, bt), bt), :] = o_f
        jb = nb - 1 - j
        o_b, st_b = _hgrn_block(jb, True, st_b, aq_ref, ai_ref, afb_ref, lb_b, cum_b, rest_b)
        ob_scr[pl.ds(pl.multiple_of(jb * bt, bt), bt), :] = o_b
        return st_f, st_b

    st_f, st_b = lax.fori_loop(0, nb, body, (st_f0, st_b0))

    o = of_scr[...] + ob_scr[...]
    o = o * lax.rsqrt(jnp.mean(o * o, axis=-1, keepdims=True) + EPS) * gn_ref[...]
    o_ref[...] = (o * _silu(ag_ref[...])).astype(o_ref.dtype)
    if emit_state:
        st_ref[0] = st_f.T
        st_ref[1] = st_b.T


def _hgrn(pa3, lb_logits4, a_norm3, s0, emit_state):
    b, t, _ = pa3.shape
    has_s0 = s0 is not None

    def seg(sidx):
        return pl.BlockSpec((None, t, A_DK), lambda bi, h: (bi, 0, sidx * A_HEADS + h))

    in_specs = [seg(0), seg(1), seg(2), seg(3), seg(4),
                pl.BlockSpec((2, None, lb_logits4.shape[2], A_DK), lambda bi, h: (0, h, 0, 0)),
                pl.BlockSpec((None, 1, A_DV), lambda bi, h: (h, 0, 0))]
    args = [pa3, pa3, pa3, pa3, pa3, lb_logits4, a_norm3]
    if has_s0:
        in_specs.append(pl.BlockSpec((None, None, 2, None, A_DK, A_DV), lambda bi, h: (bi, 0, 0, h, 0, 0)))
        args.append(s0)
    out_specs = [pl.BlockSpec((None, t, A_DV), lambda bi, h: (bi, 0, h))]
    out_shape = [jax.ShapeDtypeStruct((b, t, A_WIDTH), BF16)]
    if emit_state:
        out_specs.append(pl.BlockSpec((None, None, 2, None, A_DK, A_DV), lambda bi, h: (bi, 0, 0, h, 0, 0)))
        out_shape.append(jax.ShapeDtypeStruct((b, 1, 2, A_HEADS, A_DK, A_DV), F32))
    return pl.pallas_call(
        functools.partial(_hgrn_kernel, seq=t, has_s0=has_s0, emit_state=emit_state),
        grid=(b, A_HEADS),
        in_specs=in_specs,
        out_specs=out_specs,
        out_shape=out_shape,
        scratch_shapes=[pltpu.VMEM((t, A_DV), F32), pltpu.VMEM((t, A_DV), F32)],
        compiler_params=_cparams(("arbitrary", "arbitrary")),
        name="hgrn_state" if emit_state else "hgrn",
    )(*args)


def _dup_half(x, kv, lane):
    swapped = pltpu.roll(x, B_HEAD_DIM, axis=1)
    first = lane < B_HEAD_DIM
    return jnp.where(first, x, swapped) if kv == 0 else jnp.where(first, swapped, x)


def _attend_heads(q_of_pair, key_sets, sink_ref, o_ref, lane):
    scale = B_HEAD_DIM ** -0.5
    group = B_HEADS // B_KV_HEADS
    first = lane < B_HEAD_DIM
    for kv in range(B_KV_HEADS):
        kvs = [(_dup_half(k, kv, lane).astype(BF16), _dup_half(v, kv, lane).astype(BF16), m)
               for k, v, m in key_sets]
        for jj in range(group // 2):
            j = kv * (group // 2) + jj
            qp = q_of_pair(j)
            halves = []
            for half in range(2):
                h = 2 * j + half
                qm = jnp.where(first if half == 0 else jnp.logical_not(first), qp, jnp.zeros_like(qp))
                qm = qm.astype(BF16)
                sink = sink_ref[h]
                ss = []
                for k2, _, m in kvs:
                    s = _dot_nt(qm, k2) * scale
                    if m is not None:
                        s = jnp.where(m, s, -jnp.inf)
                    ss.append(s)
                mx = sink
                for s in ss:
                    mx = jnp.maximum(mx, jnp.max(s, axis=-1, keepdims=True))
                es = [jnp.exp(s - mx) for s in ss]
                den = jnp.exp(sink - mx)
                for e in es:
                    den = den + jnp.sum(e, axis=-1, keepdims=True)
                inv = 1.0 / den
                oh = None
                for e, (_, v2, _) in zip(es, kvs):
                    part = jnp.dot((e * inv).astype(BF16), v2, preferred_element_type=F32)
                    oh = part if oh is None else oh + part
                halves.append(oh)
            o_ref[:, j * LANES:(j + 1) * LANES] = jnp.where(first, halves[0], halves[1]).astype(o_ref.dtype)


def _ctx_attn_kernel(sink_ref, q_ref, kv_ref, o_ref):
    lane = lax.broadcasted_iota(jnp.int32, (1, LANES), 1)
    kvv = kv_ref[...]
    key_sets = [(kvv[:, :LANES], kvv[:, LANES:], None)]
    _attend_heads(lambda j: q_ref[:, j * LANES:(j + 1) * LANES], key_sets, sink_ref, o_ref, lane)


def _ctx_attn(pb3, sink):
    b, t, _ = pb3.shape
    return pl.pallas_call(
        _ctx_attn_kernel,
        grid=(b,),
        in_specs=[pl.BlockSpec(memory_space=pltpu.SMEM),
                  pl.BlockSpec((None, t, B_WIDTH), lambda bi: (bi, 0, 0)),
                  pl.BlockSpec((None, t, 2 * LANES), lambda bi: (bi, 0, B_WIDTH // (2 * LANES)))],
        out_specs=pl.BlockSpec((None, t, B_WIDTH), lambda bi: (bi, 0, 0)),
        out_shape=jax.ShapeDtypeStruct((b, t, B_WIDTH), BF16),
        compiler_params=_cparams(("arbitrary",)),
        name="ctx_attn",
    )(sink, pb3, pb3)


def _rope(x, cos, sin_signed, lane):
    nf = B_HEAD_DIM // 4
    up = pltpu.roll(x, LANES - nf, axis=1)
    down = pltpu.roll(x, nf, axis=1)
    partner = jnp.where((lane % (2 * nf)) < nf, up, down)
    return x * cos + partner * sin_signed


def _lat_attn_kernel(sink_ref, q_ref, kvl_ref, kvc_ref, kvr_ref, ck_ref, cv_ref,
                     cl_ref, cc_ref, cr_ref, sl_ref, sc_ref, sr_ref, o_ref, *, seq):
    n = pl.program_id(1)
    blk = B_BLOCK
    lane = lax.broadcasted_iota(jnp.int32, (1, LANES), 1)
    kw = jnp.concatenate([kvl_ref[:, :LANES], kvc_ref[:, :LANES], kvr_ref[:, :LANES]], axis=0)
    vw = jnp.concatenate([kvl_ref[:, LANES:], kvc_ref[:, LANES:], kvr_ref[:, LANES:]], axis=0)
    cos_w = jnp.concatenate([cl_ref[...], cc_ref[...], cr_ref[...]], axis=0)
    sin_w = jnp.concatenate([sl_ref[...], sc_ref[...], sr_ref[...]], axis=0)
    kw = _rope(kw, cos_w, sin_w, lane)
    qpos = n * blk + lax.broadcasted_iota(jnp.int32, (blk, 3 * blk), 0)
    kpos = (n - 1) * blk + lax.broadcasted_iota(jnp.int32, (blk, 3 * blk), 1)
    mask = (jnp.abs(kpos - qpos) <= WINDOW) & (kpos >= 0) & (kpos < seq)
    key_sets = [(kw, vw, mask), (ck_ref[...], cv_ref[...], None)]
    cos_c = cc_ref[...]
    sin_c = sc_ref[...]

    def q_of_pair(j):
        return _rope(q_ref[:, j * LANES:(j + 1) * LANES], cos_c, sin_c, lane)

    _attend_heads(q_of_pair, key_sets, sink_ref, o_ref, lane)


def _lat_attn(pb3, ck, cv, cos_t, sin_t, sink):
    b, t, _ = pb3.shape
    nb = t // B_BLOCK
    lc = ck.shape[1]
    kv_col = B_WIDTH // (2 * LANES)

    def kv_spec(off):
        return pl.BlockSpec((None, B_BLOCK, 2 * LANES),
                            lambda bi, n: (bi, jnp.clip(n + off, 0, nb - 1), kv_col))

    def tab_spec(off):
        return pl.BlockSpec((B_BLOCK, LANES), lambda bi, n: (jnp.clip(n + off, 0, nb - 1), 0))

    return pl.pallas_call(
        functools.partial(_lat_attn_kernel, seq=t),
        grid=(b, nb),
        in_specs=[pl.BlockSpec(memory_space=pltpu.SMEM),
                  pl.BlockSpec((None, B_BLOCK, B_WIDTH), lambda bi, n: (bi, n, 0)),
                  kv_spec(-1), kv_spec(0), kv_spec(1),
                  pl.BlockSpec((None, lc, LANES), lambda bi, n: (bi, 0, 0)),
                  pl.BlockSpec((None, lc, LANES), lambda bi, n: (bi, 0, 0)),
                  tab_spec(-1), tab_spec(0), tab_spec(1),
                  tab_spec(-1), tab_spec(0), tab_spec(1)],
        out_specs=pl.BlockSpec((None, B_BLOCK, B_WIDTH), lambda bi, n: (bi, n, 0)),
        out_shape=jax.ShapeDtypeStruct((b, t, B_WIDTH), BF16),
        compiler_params=_cparams(("arbitrary", "arbitrary")),
        name="lat_attn",
    )(sink, pb3, pb3, pb3, pb3, ck, cv, cos_t, cos_t, cos_t, sin_t, sin_t, sin_t)


def _rope_tables(seq):
    half = B_HEAD_DIM // 2
    nf = half // 2
    row = jnp.repeat(jnp.arange(seq // GRID_W), GRID_W).astype(F32)
    col = jnp.tile(jnp.arange(GRID_W), seq // GRID_W).astype(F32)
    inv = ROPE_BASE ** (-jnp.arange(nf, dtype=F32) / nf)
    ang_r = row[:, None] * inv[None, :]
    ang_c = col[:, None] * inv[None, :]

    def part(ang):
        c = jnp.cos(ang)
        s = jnp.sin(ang)
        return jnp.concatenate([c, c], axis=-1), jnp.concatenate([-s, s], axis=-1)

    cr, sr = part(ang_r)
    cc, sc = part(ang_c)
    cos_h = jnp.concatenate([cr, cc], axis=-1)
    sin_h = jnp.concatenate([sr, sc], axis=-1)
    reps = LANES // B_HEAD_DIM
    return jnp.tile(cos_h, (1, reps)), jnp.tile(sin_h, (1, reps))


def _postmix_kernel(oa_ref, ob_ref, ga_ref, gb_ref, x_ref, gt1_ref, sh2_ref, sc2_ref, g2_ref,
                    woa_ref, wob_ref, wout_ref, wq_ref, keys_ref, x1_ref, h2_ref, s_ref):
    ya = jnp.dot(oa_ref[...], woa_ref[...], preferred_element_type=F32)
    yb = jnp.dot(ob_ref[...], wob_ref[...], preferred_element_type=F32)
    u = _sigmoid(ga_ref[...]) * ya + _sigmoid(gb_ref[...]) * yb
    y = jnp.dot(u.astype(BF16), wout_ref[...], preferred_element_type=F32)
    x1 = x_ref[...] + gt1_ref[...] * y
    x1_ref[...] = x1
    h2 = _rmsnorm(x1, g2_ref[...]) * (1.0 + sc2_ref[...]) + sh2_ref[...]
    h2b = h2.astype(BF16)
    h2_ref[...] = h2b
    q = jnp.dot(h2b, wq_ref[...], preferred_element_type=F32)
    for hp in range(2 * P_HEADS):
        cs = slice(hp * LANES, (hp + 1) * LANES)
        s_ref[hp] = _dot_nt(keys_ref[hp], q[:, cs].astype(BF16))


def _postmix(oa, ob, pg, x2, mod3, g2, woa, wob, wout, wq, keys, row_fn, tm=256):
    t = x2.shape[0]
    nhp, nkeys = keys.shape[0], keys.shape[1]
    full = lambda shape: pl.BlockSpec(shape, lambda i: (0,) * len(shape))
    return pl.pallas_call(
        _postmix_kernel,
        grid=(t // tm,),
        in_specs=[pl.BlockSpec((tm, A_WIDTH), lambda i: (i, 0)),
                  pl.BlockSpec((tm, B_WIDTH), lambda i: (i, 0)),
                  pl.BlockSpec((tm, D_MODEL), lambda i: (i, 0)),
                  pl.BlockSpec((tm, D_MODEL), lambda i: (i, 1)),
                  pl.BlockSpec((tm, D_MODEL), lambda i: (i, 0)),
                  _mod_spec(2, row_fn, tm), _mod_spec(3, row_fn, tm), _mod_spec(4, row_fn, tm),
                  full((1, D_MODEL)),
                  full(woa.shape), full(wob.shape), full(wout.shape), full(wq.shape), full(keys.shape)],
        out_specs=[pl.BlockSpec((tm, D_MODEL), lambda i: (i, 0)),
                   pl.BlockSpec((tm, D_MODEL), lambda i: (i, 0)),
                   pl.BlockSpec((nhp, nkeys, tm), lambda i: (0, 0, i))],
        out_shape=[jax.ShapeDtypeStruct((t, D_MODEL), F32),
                   jax.ShapeDtypeStruct((t, D_MODEL), BF16),
                   jax.ShapeDtypeStruct((nhp, nkeys, t), F32)],
        compiler_params=_cparams(("arbitrary",)),
        name="postmix",
    )(oa, ob, pg, pg, x2, mod3, mod3, mod3, g2, woa, wob, wout, wq, keys)


def _extract_max(vals, order, big):
    m = jnp.max(vals, axis=0, keepdims=True)
    first = jnp.min(jnp.where(vals == m, order, big), axis=0, keepdims=True)
    return m, first, order == first


def _topk_kernel(s_ref, e1_ref, e2_ref, g_ref):
    tl = s_ref.shape[2]
    nk = P_NKEYS
    kk = P_TOPK
    neg = -jnp.inf
    key_iota = lax.broadcasted_iota(jnp.int32, (nk, tl), 0).astype(F32)
    rank = lax.broadcasted_iota(jnp.int32, (kk, tl), 0)
    gs, es = [], []
    for h in range(P_HEADS):
        tops = []
        for side in range(2):
            s = s_ref[2 * h + side]
            v = jnp.zeros((kk, tl), F32)
            ix = jnp.zeros((kk, tl), F32)
            for a in range(kk):
                m, i, hit = _extract_max(s, key_iota, float(nk))
                s = jnp.where(hit, neg, s)
                v = jnp.where(rank == a, m, v)
                ix = jnp.where(rank == a, i, ix)
            tops.append((v, ix))
        (v1, i1), (v2, i2) = tops
        cands, eids, flats = [], [], []
        for a in range(kk // 2):
            nb = kk // (a + 1)
            nr = -(-nb // SUBLANES) * SUBLANES
            r = lax.broadcasted_iota(jnp.int32, (nr, tl), 0)
            c = v1[a:a + 1] + v2[:nr]
            cands.append(c if nb == nr else jnp.where(r < nb, c, neg))
            eids.append(i1[a:a + 1] * float(nk) + i2[:nr])
            flats.append((r + a * kk).astype(F32))
        r = lax.broadcasted_iota(jnp.int32, (kk // 2, tl), 0)
        cands.append(v1[kk // 2:] + v2[0:1])
        eids.append(i1[kk // 2:] * float(nk) + i2[0:1])
        flats.append(((r + kk // 2) * kk).astype(F32))
        cand = jnp.concatenate(cands, axis=0)
        eid = jnp.concatenate(eids, axis=0)
        flat = jnp.concatenate(flats, axis=0)
        ts = jnp.zeros((kk, tl), F32)
        te = jnp.zeros((kk, tl), F32)
        for k in range(kk):
            m, _, hit = _extract_max(cand, flat, float(kk * kk))
            e = jnp.max(jnp.where(hit, eid, -1.0), axis=0, keepdims=True)
            cand = jnp.where(hit, neg, cand)
            ts = jnp.where(rank == k, m, ts)
            te = jnp.where(rank == k, e, te)
        p = jnp.exp(ts - ts[0:1])
        gs.append(p / jnp.sum(p, axis=0, keepdims=True))
        es.append(te)
    g_ref[...] = jnp.concatenate(gs, axis=0).T
    e_all = jnp.concatenate(es, axis=0).T.astype(jnp.int32)
    e1_ref[...] = lax.shift_right_logical(e_all, nk.bit_length() - 1)
    e2_ref[...] = e_all & (nk - 1)


def _topk(scores, tl=128):
    nhp, nkeys, t = scores.shape
    w = P_HEADS * P_TOPK
    return pl.pallas_call(
        _topk_kernel,
        grid=(t // tl,),
        in_specs=[pl.BlockSpec((nhp, nkeys, tl), lambda i: (0, 0, i))],
        out_specs=[pl.BlockSpec((tl, w), lambda i: (i, 0))] * 3,
        out_shape=[jax.ShapeDtypeStruct((t, w), jnp.int32),
                   jax.ShapeDtypeStruct((t, w), jnp.int32),
                   jax.ShapeDtypeStruct((t, w), F32)],
        compiler_params=_cparams(("arbitrary",)),
        name="topk",
    )(scores)


def _gelu(x):
    return 0.5 * x * (1.0 + lax.erf(x * (2.0 ** -0.5)))


def _peer_kernel(h2_ref, e1_ref, e2_ref, g_ref, wu_ref, wv_ref, x1_ref, gt2_ref, nf_ref, o_ref,
                 gate_scr, acc_scr):
    e = pl.program_id(1)
    tm = h2_ref.shape[0]
    nk = P_NKEYS

    @pl.when(e == 0)
    def _build_gates():
        acc_scr[...] = jnp.zeros_like(acc_scr)
        sub = lax.broadcasted_iota(jnp.int32, (nk, P_HEADS * P_TOPK), 0)

        def per_token(t, carry):
            a = e1_ref[pl.ds(t, 1), :]
            b = e2_ref[pl.ds(t, 1), :]
            w = g_ref[pl.ds(t, 1), :]
            pt = jnp.where(sub == a, w, 0.0).astype(BF16)
            qt = jnp.where(sub == b, 1.0, 0.0).astype(BF16)
            gate_scr[pl.ds(pl.multiple_of(t * G_PITCH, SUBLANES), nk), :] = _dot_nt(pt, qt)
            return carry

        lax.fori_loop(0, tm, per_token, 0, unroll=SUBLANES)

    h2 = h2_ref[...]
    part = None
    for c in range(PEER_E1 // 2):
        rows = slice(c * 2 * nk, (c + 1) * 2 * nk)
        a = _dot_nt(h2, wu_ref[rows, :])
        gates = [gate_scr[pl.ds(e * PEER_E1 + 2 * c + r, tm, stride=G_PITCH), :] for r in range(2)]
        hid = (jnp.concatenate(gates, axis=1) * _gelu(a)).astype(BF16)
        d = jnp.dot(hid, wv_ref[rows, :], preferred_element_type=F32)
        part = d if part is None else part + d
    acc_scr[...] += part

    @pl.when(e == pl.num_programs(1) - 1)
    def _finish():
        x2 = x1_ref[...] + gt2_ref[...] * acc_scr[...]
        o_ref[...] = _rmsnorm(x2, nf_ref[...])


def _peer(h2, e1, e2, g, wu, wv, x1, mod3, norm_f, row_fn, tm=PEER_TM):
    t = h2.shape[0]
    te = PEER_E1 * P_NKEYS
    ne = wu.shape[0] // te
    return pl.pallas_call(
        _peer_kernel,
        grid=(t // tm, ne),
        in_specs=[pl.BlockSpec((tm, D_MODEL), lambda i, e: (i, 0)),
                  pl.BlockSpec((tm, LANES), lambda i, e: (i, 0)),
                  pl.BlockSpec((tm, LANES), lambda i, e: (i, 0)),
                  pl.BlockSpec((tm, LANES), lambda i, e: (i, 0)),
                  pl.BlockSpec((te, D_MODEL), lambda i, e: (e, 0)),
                  pl.BlockSpec((te, D_MODEL), lambda i, e: (e, 0)),
                  pl.BlockSpec((tm, D_MODEL), lambda i, e: (i, 0)),
                  _mod_spec(5, row_fn, tm),
                  pl.BlockSpec((1, D_MODEL), lambda i, e: (0, 0))],
        out_specs=pl.BlockSpec((tm, D_MODEL), lambda i, e: (i, 0)),
        out_shape=jax.ShapeDtypeStruct((t, D_MODEL), F32),
        scratch_shapes=[pltpu.VMEM((tm * G_PITCH, LANES), F32), pltpu.VMEM((tm, D_MODEL), F32)],
        compiler_params=_cparams(("arbitrary", "arbitrary")),
        name="peer",
    )(h2, e1, e2, g, wu, wv, x1, mod3, norm_f)


def _stream(x, mod3, row_fn, wts, s0, ctx_kv, emit_state):
    b, t, d = x.shape
    x2 = x.reshape(b * t, d)
    pa, pb, pg = _inproj(x2, wts["g1"], mod3, wts["w_in"], row_fn)
    pa3 = pa.reshape(b, t, PA_W)
    pb3 = pb.reshape(b, t, PB_W)
    res = _hgrn(pa3, wts["lb_logits"], wts["a_norm"], s0, emit_state)
    oa = res[0].reshape(b * t, A_WIDTH)
    states = res[1] if emit_state else None
    if ctx_kv is None:
        ob = _ctx_attn(pb3, wts["sink"])
    else:
        cos_t, sin_t = _rope_tables(t)
        ob = _lat_attn(pb3, ctx_kv[0], ctx_kv[1], cos_t, sin_t, wts["sink"])
    ob = ob.reshape(b * t, B_WIDTH)
    x1, h2, scores = _postmix(oa, ob, pg, x2, mod3, wts["g2"], wts["w_oa"], wts["w_ob"], wts["w_out"],
                              wts["p_wq"], wts["p_keys"], row_fn)
    e1, e2, g = _topk(scores)
    y = _peer(h2, e1, e2, g, wts["p_u"], wts["p_v"], x1, mod3, wts["norm_f"], row_fn)
    return y.reshape(b, t, d), states, pb3


def kernel(x_prompt, x_sample, state_hgrn, cache_k, cache_v, c, c_ctx, w_ada, b_ada, norm1, norm2,
           w_in, lb_logits, a_norm, b_sink, w_oa, w_ob, w_out, p_wq, p_keys, p_u, p_v, norm_f):
    depth = w_ada.shape[0]
    assert depth == 1, "single-layer step"
    bp, seq, d = x_prompt.shape
    bs, dseq, _ = x_sample.shape
    kvw = B_KV_HEADS * B_HEAD_DIM

    nrow = -(-(bs + 1) // SUBLANES) * SUBLANES
    cond = jnp.zeros((nrow, d), F32).at[:bs].set(c).at[bs].set(c_ctx)
    mod = _ada(cond, w_ada[0], b_ada[0].reshape(1, -1))
    mod3 = mod.reshape(nrow, 1, 6 * d)

    wts = dict(
        g1=norm1[0].reshape(1, d), g2=norm2[0].reshape(1, d), norm_f=norm_f.reshape(1, d),
        w_in=w_in[0].astype(BF16),
        lb_logits=lb_logits.reshape(2, depth + 1, A_HEADS, A_DK).transpose(0, 2, 1, 3),
        a_norm=a_norm[0].reshape(A_HEADS, 1, A_DV),
        sink=b_sink[0],
        w_oa=w_oa[0].astype(BF16), w_ob=w_ob[0].astype(BF16), w_out=w_out[0].astype(BF16),
        p_wq=p_wq[0].astype(BF16),
        p_keys=p_keys[0].reshape(2 * P_HEADS, P_NKEYS, P_QDIM // 2).astype(BF16),
        p_u=p_u[0].astype(BF16), p_v=p_v[0].astype(BF16),
    )

    y_prompt, states, pb3 = _stream(x_prompt, mod3, lambda tok: bs, wts, None, None, True)
    ck = cache_k[:, 0].reshape(bs, -1, kvw)
    cv = cache_v[:, 0].reshape(bs, -1, kvw)
    y_sample, _, _ = _stream(x_sample, mod3, lambda tok: tok // dseq, wts, state_hgrn, (ck, cv), False)

    new_k = pb3[:, :, B_WIDTH:B_WIDTH + kvw].reshape(bp, 1, seq, B_KV_HEADS, B_HEAD_DIM)
    new_v = pb3[:, :, B_WIDTH + kvw:].reshape(bp, 1, seq, B_KV_HEADS, B_HEAD_DIM)
    return (y_prompt, y_sample, states.astype(x_prompt.dtype), new_k, new_v)
```

```python
import functools

import jax
import jax.numpy as jnp
import numpy as np
from jax import lax
from jax.experimental import pallas as pl
from jax.experimental.pallas import tpu as pltpu

F32 = jnp.float32
BF16 = jnp.bfloat16
HIGHEST = lax.Precision.HIGHEST
LOG2E = 1.4426950408889634

D_MODEL = 1024
GRID_W = 64
EPS = 1e-6
A_HEADS = 4
A_DK = 128
A_DV = 128
A_WIDTH = A_HEADS * A_DV
A_CHUNK = 16
B_HEADS = 8
B_KV_HEADS = 2
B_HEAD_DIM = 64
B_WIDTH = B_HEADS * B_HEAD_DIM
WINDOW = 128
B_BLOCK = 128
ROPE_BASE = 10000.0
P_HEADS = 8
P_NKEYS = 128
P_QDIM = 256
P_TOPK = 16

PA_W = 5 * A_WIDTH
PB_W = B_WIDTH + 2 * B_KV_HEADS * B_HEAD_DIM
PG_W = 2 * D_MODEL

LANES = 128
SUBLANES = 8
VMEM_LIMIT = 60 * 1024 * 1024

PEER_TM = 512
PEER_E1 = 4
G_PITCH = P_NKEYS + SUBLANES


def _cparams(sem):
    return pltpu.CompilerParams(dimension_semantics=sem, vmem_limit_bytes=VMEM_LIMIT)


def _sigmoid(x):
    return jax.nn.sigmoid(x)


def _silu(x):
    return x * jax.nn.sigmoid(x)


def _rmsnorm(x, g):
    return x * lax.rsqrt(jnp.mean(x * x, axis=-1, keepdims=True) + EPS) * g


def _dot_nt(a, b, **kw):
    return lax.dot_general(a, b, (((1,), (1,)), ((), ())), preferred_element_type=F32, **kw)


def _ada_kernel(c_ref, w_ref, b_ref, o_ref):
    s = _silu(c_ref[...])
    o_ref[...] = jnp.dot(s, w_ref[...], precision=HIGHEST, preferred_element_type=F32) + b_ref[...]


def _ada(cond, w, b):
    n = cond.shape[0]
    nc = w.shape[1] // D_MODEL
    return pl.pallas_call(
        _ada_kernel,
        grid=(nc,),
        in_specs=[pl.BlockSpec((n, D_MODEL), lambda j: (0, 0)),
                  pl.BlockSpec((D_MODEL, D_MODEL), lambda j: (0, j)),
                  pl.BlockSpec((1, D_MODEL), lambda j: (0, j))],
        out_specs=pl.BlockSpec((n, D_MODEL), lambda j: (0, j)),
        out_shape=jax.ShapeDtypeStruct((n, w.shape[1]), F32),
        compiler_params=_cparams(("arbitrary",)),
        name="ada",
    )(cond, w, b)


def _mod_spec(chunk, row_fn, tm):
    return pl.BlockSpec((None, 1, D_MODEL), lambda i, *_: (row_fn(i * tm), 0, chunk))


def _inproj_kernel(x_ref, g_ref, sh_ref, sc_ref, w_ref, pa_ref, pb_ref, pg_ref):
    h = _rmsnorm(x_ref[...], g_ref[...]) * (1.0 + sc_ref[...]) + sh_ref[...]
    hb = h.astype(BF16)
    pa_ref[...] = jnp.dot(hb, w_ref[:, :PA_W], preferred_element_type=F32)
    pb_ref[...] = jnp.dot(hb, w_ref[:, PA_W:PA_W + PB_W], preferred_element_type=F32)
    pg_ref[...] = jnp.dot(hb, w_ref[:, PA_W + PB_W:], preferred_element_type=F32)


def _inproj(x2, g1, mod3, w_in_bf, row_fn, tm=256):
    t = x2.shape[0]
    d_in = w_in_bf.shape[1]
    return pl.pallas_call(
        _inproj_kernel,
        grid=(t // tm,),
        in_specs=[pl.BlockSpec((tm, D_MODEL), lambda i: (i, 0)),
                  pl.BlockSpec((1, D_MODEL), lambda i: (0, 0)),
                  _mod_spec(0, row_fn, tm), _mod_spec(1, row_fn, tm),
                  pl.BlockSpec((D_MODEL, d_in), lambda i: (0, 0))],
        out_specs=[pl.BlockSpec((tm, PA_W), lambda i: (i, 0)),
                   pl.BlockSpec((tm, PB_W), lambda i: (i, 0)),
                   pl.BlockSpec((tm, PG_W), lambda i: (i, 0))],
        out_shape=[jax.ShapeDtypeStruct((t, PA_W), F32),
                   jax.ShapeDtypeStruct((t, PB_W), F32),
                   jax.ShapeDtypeStruct((t, PG_W), F32)],
        compiler_params=_cparams(("arbitrary",)),
        name="inproj",
    )(x2, g1, mod3, mod3, w_in_bf)


def _hgrn_intra(bc2, q, bc_ref, k_ref, v_ref, base, c0, reverse):
    c = A_CHUNK
    half = SUBLANES
    row = lax.broadcasted_iota(jnp.int32, (c, 1), 0)
    acc = [jnp.zeros((half, A_DV), F32), jnp.zeros((half, A_DV), F32)]
    for s in range(c):
        if reverse:
            parts = [0] if s < half else [0, 1]
        else:
            parts = [0, 1] if s < half else [1]
        bcs = bc_ref[c0 + s:c0 + s + 1, :]
        ks = k_ref[c0 + s:c0 + s + 1, :]
        vs = v_ref[pl.ds(base + c0 + s, 1), :]
        for p in parts:
            sl = slice(p * half, (p + 1) * half)
            d = bc2[sl] - bcs
            if (s // half) == p:
                keep = (row[sl] <= s) if reverse else (row[sl] >= s)
                d = jnp.where(keep, d, -jnp.inf)
            w = jnp.exp2(d) * (q[sl] * ks)
            sc = jnp.sum(w, axis=-1, keepdims=True)
            acc[p] = acc[p] + sc * vs
    return jnp.concatenate(acc, axis=0)


def _split3(x):
    hi = x.astype(BF16)
    r1 = x - hi.astype(F32)
    mid = r1.astype(BF16)
    lo = (r1 - mid.astype(F32)).astype(BF16)
    return hi, mid, lo


def _hgrn_block(blk, reverse, st, aq_ref, ai_ref, af_ref, lb, sums_m, bc_ref, k_ref):
    bt = LANES
    base = pl.multiple_of(blk * bt, bt)
    rows = pl.ds(base, bt)
    q = _silu(aq_ref[rows, :])
    v = ai_ref[rows, :]
    f = lb + (1.0 - lb) * _sigmoid(af_ref[rows, :])
    k = 1.0 - f
    lf2 = jnp.log(f) * LOG2E
    sums = None
    for piece in _split3(lf2):
        d = jnp.dot(sums_m, piece, preferred_element_type=F32)
        sums = d if sums is None else sums + d
    bc2 = sums[:bt]
    rest2 = sums[bt:]
    bc_ref[...] = bc2
    k_ref[...] = k
    qe = (q * jnp.exp2(bc2)).astype(BF16)
    ke = (k * jnp.exp2(rest2)).astype(BF16)
    ebt = jnp.exp2(bc2 + rest2)
    vt = v.T.astype(BF16)
    tok = lax.broadcasted_iota(jnp.int32, (bt, 1), 0) // A_CHUNK
    nchunk = bt // A_CHUNK
    kem = jnp.concatenate([jnp.where(tok == c, ke, jnp.zeros_like(ke)) for c in range(nchunk)], axis=1)
    ds_all = jnp.dot(vt, kem, preferred_element_type=F32)
    starts = [None] * nchunk
    order = range(nchunk - 1, -1, -1) if reverse else range(nchunk)
    for c in order:
        starts[c] = st.astype(BF16)
        st = st * ebt[c * A_CHUNK:c * A_CHUNK + 1, :] + ds_all[:, c * A_DK:(c + 1) * A_DK]
    outs = []
    for c in range(nchunk):
        cs = slice(c * A_CHUNK, (c + 1) * A_CHUNK)
        o_inter = _dot_nt(qe[cs], starts[c])
        o_intra = _hgrn_intra(bc2[cs], q[cs], bc_ref, k_ref, ai_ref, base, c * A_CHUNK, reverse)
        outs.append(o_inter + o_intra)
    return jnp.concatenate(outs, axis=0), st


def _hgrn_kernel(*refs, seq, has_s0, emit_state):
    aq_ref, ai_ref, aff_ref, afb_ref, ag_ref, lbl_ref, gn_ref = refs[:7]
    pos = 7
    s0_ref = None
    if has_s0:
        s0_ref = refs[pos]
        pos += 1
    o_ref = refs[pos]
    pos += 1
    st_ref = None
    if emit_state:
        st_ref = refs[pos]
        pos += 1
    of_scr, ob_scr, bcf_scr, kf_scr, bcb_scr, kb_scr = refs[pos:pos + 6]

    bt = LANES
    nb = seq // bt
    lbl = lbl_ref[...]

    def lower_bound(d):
        z = lbl[d]
        m = jnp.max(z, axis=0, keepdims=True)
        e = jnp.exp(z - m)
        return e[0:1, :] / jnp.sum(e, axis=0, keepdims=True)

    lb_f = lower_bound(0)
    lb_b = lower_bound(1)

    r = lax.broadcasted_iota(jnp.int32, (bt, bt), 0)
    cidx = lax.broadcasted_iota(jnp.int32, (bt, bt), 1)
    same = (r // A_CHUNK) == (cidx // A_CHUNK)
    one = jnp.ones((bt, bt), F32)
    zero = jnp.zeros((bt, bt), F32)

    def sums_matrix(cum_mask, rest_mask):
        return jnp.concatenate([jnp.where(same & cum_mask, one, zero),
                                jnp.where(same & rest_mask, one, zero)], axis=0).astype(BF16)

    sums_f = sums_matrix(cidx <= r, cidx > r)
    sums_b = sums_matrix(cidx >= r, cidx < r)

    if has_s0:
        st_f0 = s0_ref[0].T
        st_b0 = s0_ref[1].T
    else:
        st_f0 = jnp.zeros((A_DV, A_DK), F32)
        st_b0 = jnp.zeros((A_DV, A_DK), F32)

    def body(j, carry):
        st_f, st_b = carry
        o_f, st_f = _hgrn_block(j, False, st_f, aq_ref, ai_ref, aff_ref, lb_f, sums_f, bcf_scr, kf_scr)
        of_scr[pl.ds(pl.multiple_of(j * bt, bt), bt), :] = o_f
        jb = nb - 1 - j
        o_b, st_b = _hgrn_block(jb, True, st_b, aq_ref, ai_ref, afb_ref, lb_b, sums_b, bcb_scr, kb_scr)
        ob_scr[pl.ds(pl.multiple_of(jb * bt, bt), bt), :] = o_b
        return st_f, st_b

    st_f, st_b = lax.fori_loop(0, nb, body, (st_f0, st_b0))

    o = of_scr[...] + ob_scr[...]
    o = o * lax.rsqrt(jnp.mean(o * o, axis=-1, keepdims=True) + EPS) * gn_ref[...]
    o_ref[...] = (o * _silu(ag_ref[...])).astype(o_ref.dtype)
    if emit_state:
        st_ref[0] = st_f.T
        st_ref[1] = st_b.T


def _hgrn(pa3, lb_logits4, a_norm3, s0, emit_state):
    b, t, _ = pa3.shape
    has_s0 = s0 is not None

    def seg(sidx):
        return pl.BlockSpec((None, t, A_DK), lambda bi, h: (bi, 0, sidx * A_HEADS + h))

    in_specs = [seg(0), seg(1), seg(2), seg(3), seg(4),
                pl.BlockSpec((2, None, lb_logits4.shape[2], A_DK), lambda bi, h: (0, h, 0, 0)),
                pl.BlockSpec((None, 1, A_DV), lambda bi, h: (h, 0, 0))]
    args = [pa3, pa3, pa3, pa3, pa3, lb_logits4, a_norm3]
    if has_s0:
        in_specs.append(pl.BlockSpec((None, None, 2, None, A_DK, A_DV), lambda bi, h: (bi, 0, 0, h, 0, 0)))
        args.append(s0)
    out_specs = [pl.BlockSpec((None, t, A_DV), lambda bi, h: (bi, 0, h))]
    out_shape = [jax.ShapeDtypeStruct((b, t, A_WIDTH), BF16)]
    if emit_state:
        out_specs.append(pl.BlockSpec((None, None, 2, None, A_DK, A_DV), lambda bi, h: (bi, 0, 0, h, 0, 0)))
        out_shape.append(jax.ShapeDtypeStruct((b, 1, 2, A_HEADS, A_DK, A_DV), F32))
    return pl.pallas_call(
        functools.partial(_hgrn_kernel, seq=t, has_s0=has_s0, emit_state=emit_state),
        grid=(b, A_HEADS),
        in_specs=in_specs,
        out_specs=out_specs,
        out_shape=out_shape,
        scratch_shapes=[pltpu.VMEM((t, A_DV), F32), pltpu.VMEM((t, A_DV), F32)]
        + [pltpu.VMEM((LANES, A_DK), F32)] * 4,
        compiler_params=_cparams(("arbitrary", "arbitrary")),
        name="hgrn_state" if emit_state else "hgrn",
    )(*args)


def _dup_half(x, kv, lane):
    swapped = pltpu.roll(x, B_HEAD_DIM, axis=1)
    first = lane < B_HEAD_DIM
    return jnp.where(first, x, swapped) if kv == 0 else jnp.where(first, swapped, x)


def _attend_heads(q_of_pair, key_sets, sink_ref, o_ref, lane):
    scale = B_HEAD_DIM ** -0.5
    group = B_HEADS // B_KV_HEADS
    first = lane < B_HEAD_DIM
    for kv in range(B_KV_HEADS):
        kvs = [(_dup_half(k, kv, lane).astype(BF16), _dup_half(v, kv, lane).astype(BF16), m)
               for k, v, m in key_sets]
        for jj in range(group // 2):
            j = kv * (group // 2) + jj
            qp = q_of_pair(j)
            halves = []
            for half in range(2):
                h = 2 * j + half
                qm = jnp.where(first if half == 0 else jnp.logical_not(first), qp, jnp.zeros_like(qp))
                qm = qm.astype(BF16)
                sink = sink_ref[h]
                ss = []
                for k2, _, m in kvs:
                    s = _dot_nt(qm, k2) * scale
                    if m is not None:
                        s = jnp.where(m, s, -jnp.inf)
                    ss.append(s)
                mx = sink
                for s in ss:
                    mx = jnp.maximum(mx, jnp.max(s, axis=-1, keepdims=True))
                es = [jnp.exp(s - mx) for s in ss]
                den = jnp.exp(sink - mx)
                for e in es:
                    den = den + jnp.sum(e, axis=-1, keepdims=True)
                inv = 1.0 / den
                oh = None
                for e, (_, v2, _) in zip(es, kvs):
                    part = jnp.dot((e * inv).astype(BF16), v2, preferred_element_type=F32)
                    oh = part if oh is None else oh + part
                halves.append(oh)
            o_ref[:, j * LANES:(j + 1) * LANES] = jnp.where(first, halves[0], halves[1]).astype(o_ref.dtype)


def _ctx_attn_kernel(sink_ref, q_ref, kv_ref, o_ref):
    lane = lax.broadcasted_iota(jnp.int32, (1, LANES), 1)
    kvv = kv_ref[...]
    key_sets = [(kvv[:, :LANES], kvv[:, LANES:], None)]
    _attend_heads(lambda j: q_ref[:, j * LANES:(j + 1) * LANES], key_sets, sink_ref, o_ref, lane)


def _ctx_attn(pb3, sink):
    b, t, _ = pb3.shape
    return pl.pallas_call(
        _ctx_attn_kernel,
        grid=(b,),
        in_specs=[pl.BlockSpec(memory_space=pltpu.SMEM),
                  pl.BlockSpec((None, t, B_WIDTH), lambda bi: (bi, 0, 0)),
                  pl.BlockSpec((None, t, 2 * LANES), lambda bi: (bi, 0, B_WIDTH // (2 * LANES)))],
        out_specs=pl.BlockSpec((None, t, B_WIDTH), lambda bi: (bi, 0, 0)),
        out_shape=jax.ShapeDtypeStruct((b, t, B_WIDTH), BF16),
        compiler_params=_cparams(("arbitrary",)),
        name="ctx_attn",
    )(sink, pb3, pb3)


def _rope(x, cos, sin_signed, lane):
    nf = B_HEAD_DIM // 4
    up = pltpu.roll(x, LANES - nf, axis=1)
    down = pltpu.roll(x, nf, axis=1)
    partner = jnp.where((lane % (2 * nf)) < nf, up, down)
    return x * cos + partner * sin_signed


def _lat_attn_kernel(sink_ref, q_ref, kvl_ref, kvc_ref, kvr_ref, ck_ref, cv_ref,
                     cl_ref, cc_ref, cr_ref, sl_ref, sc_ref, sr_ref, o_ref, *, seq):
    n = pl.program_id(1)
    blk = B_BLOCK
    lane = lax.broadcasted_iota(jnp.int32, (1, LANES), 1)
    kw = jnp.concatenate([kvl_ref[:, :LANES], kvc_ref[:, :LANES], kvr_ref[:, :LANES]], axis=0)
    vw = jnp.concatenate([kvl_ref[:, LANES:], kvc_ref[:, LANES:], kvr_ref[:, LANES:]], axis=0)
    cos_w = jnp.concatenate([cl_ref[...], cc_ref[...], cr_ref[...]], axis=0)
    sin_w = jnp.concatenate([sl_ref[...], sc_ref[...], sr_ref[...]], axis=0)
    kw = _rope(kw, cos_w, sin_w, lane)
    qpos = n * blk + lax.broadcasted_iota(jnp.int32, (blk, 3 * blk), 0)
    kpos = (n - 1) * blk + lax.broadcasted_iota(jnp.int32, (blk, 3 * blk), 1)
    mask = (jnp.abs(kpos - qpos) <= WINDOW) & (kpos >= 0) & (kpos < seq)
    key_sets = [(kw, vw, mask), (ck_ref[...], cv_ref[...], None)]
    cos_c = cc_ref[...]
    sin_c = sc_ref[...]

    def q_of_pair(j):
        return _rope(q_ref[:, j * LANES:(j + 1) * LANES], cos_c, sin_c, lane)

    _attend_heads(q_of_pair, key_sets, sink_ref, o_ref, lane)


def _lat_attn(pb3, ck, cv, cos_t, sin_t, sink):
    b, t, _ = pb3.shape
    nb = t // B_BLOCK
    lc = ck.shape[1]
    kv_col = B_WIDTH // (2 * LANES)

    def kv_spec(off):
        return pl.BlockSpec((None, B_BLOCK, 2 * LANES),
                            lambda bi, n: (bi, jnp.clip(n + off, 0, nb - 1), kv_col))

    def tab_spec(off):
        return pl.BlockSpec((B_BLOCK, LANES), lambda bi, n: (jnp.clip(n + off, 0, nb - 1), 0))

    return pl.pallas_call(
        functools.partial(_lat_attn_kernel, seq=t),
        grid=(b, nb),
        in_specs=[pl.BlockSpec(memory_space=pltpu.SMEM),
                  pl.BlockSpec((None, B_BLOCK, B_WIDTH), lambda bi, n: (bi, n, 0)),
                  kv_spec(-1), kv_spec(0), kv_spec(1),
                  pl.BlockSpec((None, lc, LANES), lambda bi, n: (bi, 0, 0)),
                  pl.BlockSpec((None, lc, LANES), lambda bi, n: (bi, 0, 0)),
                  tab_spec(-1), tab_spec(0), tab_spec(1),
                  tab_spec(-1), tab_spec(0), tab_spec(1)],
        out_specs=pl.BlockSpec((None, B_BLOCK, B_WIDTH), lambda bi, n: (bi, n, 0)),
        out_shape=jax.ShapeDtypeStruct((b, t, B_WIDTH), BF16),
        compiler_params=_cparams(("arbitrary", "arbitrary")),
        name="lat_attn",
    )(sink, pb3, pb3, pb3, pb3, ck, cv, cos_t, cos_t, cos_t, sin_t, sin_t, sin_t)


def _rope_tables(seq):
    half = B_HEAD_DIM // 2
    nf = half // 2
    row = jnp.repeat(jnp.arange(seq // GRID_W), GRID_W).astype(F32)
    col = jnp.tile(jnp.arange(GRID_W), seq // GRID_W).astype(F32)
    inv = ROPE_BASE ** (-jnp.arange(nf, dtype=F32) / nf)
    ang_r = row[:, None] * inv[None, :]
    ang_c = col[:, None] * inv[None, :]

    def part(ang):
        c = jnp.cos(ang)
        s = jnp.sin(ang)
        return jnp.concatenate([c, c], axis=-1), jnp.concatenate([-s, s], axis=-1)

    cr, sr = part(ang_r)
    cc, sc = part(ang_c)
    cos_h = jnp.concatenate([cr, cc], axis=-1)
    sin_h = jnp.concatenate([sr, sc], axis=-1)
    reps = LANES // B_HEAD_DIM
    return jnp.tile(cos_h, (1, reps)), jnp.tile(sin_h, (1, reps))


def _postmix_kernel(oa_ref, ob_ref, ga_ref, gb_ref, x_ref, gt1_ref, sh2_ref, sc2_ref, g2_ref,
                    woa_ref, wob_ref, wout_ref, wq_ref, keys_ref, x1_ref, h2_ref, s_ref):
    ya = jnp.dot(oa_ref[...], woa_ref[...], preferred_element_type=F32)
    yb = jnp.dot(ob_ref[...], wob_ref[...], preferred_element_type=F32)
    u = _sigmoid(ga_ref[...]) * ya + _sigmoid(gb_ref[...]) * yb
    y = jnp.dot(u.astype(BF16), wout_ref[...], preferred_element_type=F32)
    x1 = x_ref[...] + gt1_ref[...] * y
    x1_ref[...] = x1
    h2 = _rmsnorm(x1, g2_ref[...]) * (1.0 + sc2_ref[...]) + sh2_ref[...]
    h2b = h2.astype(BF16)
    h2_ref[...] = h2b
    q = jnp.dot(h2b, wq_ref[...], preferred_element_type=F32)
    for hp in range(2 * P_HEADS):
        cs = slice(hp * LANES, (hp + 1) * LANES)
        s_ref[hp] = _dot_nt(keys_ref[hp], q[:, cs].astype(BF16))


def _postmix(oa, ob, pg, x2, mod3, g2, woa, wob, wout, wq, keys, row_fn, tm=256):
    t = x2.shape[0]
    nhp, nkeys = keys.shape[0], keys.shape[1]
    full = lambda shape: pl.BlockSpec(shape, lambda i: (0,) * len(shape))
    return pl.pallas_call(
        _postmix_kernel,
        grid=(t // tm,),
        in_specs=[pl.BlockSpec((tm, A_WIDTH), lambda i: (i, 0)),
                  pl.BlockSpec((tm, B_WIDTH), lambda i: (i, 0)),
                  pl.BlockSpec((tm, D_MODEL), lambda i: (i, 0)),
                  pl.BlockSpec((tm, D_MODEL), lambda i: (i, 1)),
                  pl.BlockSpec((tm, D_MODEL), lambda i: (i, 0)),
                  _mod_spec(2, row_fn, tm), _mod_spec(3, row_fn, tm), _mod_spec(4, row_fn, tm),
                  full((1, D_MODEL)),
                  full(woa.shape), full(wob.shape), full(wout.shape), full(wq.shape), full(keys.shape)],
        out_specs=[pl.BlockSpec((tm, D_MODEL), lambda i: (i, 0)),
                   pl.BlockSpec((tm, D_MODEL), lambda i: (i, 0)),
                   pl.BlockSpec((nhp, nkeys, tm), lambda i: (0, 0, i))],
        out_shape=[jax.ShapeDtypeStruct((t, D_MODEL), F32),
                   jax.ShapeDtypeStruct((t, D_MODEL), BF16),
                   jax.ShapeDtypeStruct((nhp, nkeys, t), F32)],
        compiler_params=_cparams(("arbitrary",)),
        name="postmix",
    )(oa, ob, pg, pg, x2, mod3, mod3, mod3, g2, woa, wob, wout, wq, keys)


def _extract_max(vals, order, big):
    m = jnp.max(vals, axis=0, keepdims=True)
    first = jnp.min(jnp.where(vals == m, order, big), axis=0, keepdims=True)
    return m, first, order == first


def _topk_kernel(s_ref, e1_ref, e2_ref, g_ref):
    tl = s_ref.shape[2]
    nk = P_NKEYS
    kk = P_TOPK
    neg = -jnp.inf
    key_iota = lax.broadcasted_iota(jnp.int32, (nk, tl), 0).astype(F32)
    rank = lax.broadcasted_iota(jnp.int32, (kk, tl), 0)
    gs, es = [], []
    for h in range(P_HEADS):
        tops = []
        for side in range(2):
            s = s_ref[2 * h + side]
            v = jnp.zeros((kk, tl), F32)
            ix = jnp.zeros((kk, tl), F32)
            for a in range(kk):
                m, i, hit = _extract_max(s, key_iota, float(nk))
                s = jnp.where(hit, neg, s)
                v = jnp.where(rank == a, m, v)
                ix = jnp.where(rank == a, i, ix)
            tops.append((v, ix))
        (v1, i1), (v2, i2) = tops
        cands, eids, flats = [], [], []
        for a in range(kk // 2):
            nb = kk // (a + 1)
            nr = -(-nb // SUBLANES) * SUBLANES
            r = lax.broadcasted_iota(jnp.int32, (nr, tl), 0)
            c = v1[a:a + 1] + v2[:nr]
            cands.append(c if nb == nr else jnp.where(r < nb, c, neg))
            eids.append(i1[a:a + 1] * float(nk) + i2[:nr])
            flats.append((r + a * kk).astype(F32))
        r = lax.broadcasted_iota(jnp.int32, (kk // 2, tl), 0)
        cands.append(v1[kk // 2:] + v2[0:1])
        eids.append(i1[kk // 2:] * float(nk) + i2[0:1])
        flats.append(((r + kk // 2) * kk).astype(F32))
        cand = jnp.concatenate(cands, axis=0)
        eid = jnp.concatenate(eids, axis=0)
        flat = jnp.concatenate(flats, axis=0)
        ts = jnp.zeros((kk, tl), F32)
        te = jnp.zeros((kk, tl), F32)
        for k in range(kk):
            m, _, hit = _extract_max(cand, flat, float(kk * kk))
            e = jnp.max(jnp.where(hit, eid, -1.0), axis=0, keepdims=True)
            cand = jnp.where(hit, neg, cand)
            ts = jnp.where(rank == k, m, ts)
            te = jnp.where(rank == k, e, te)
        p = jnp.exp(ts - ts[0:1])
        gs.append(p / jnp.sum(p, axis=0, keepdims=True))
        es.append(te)
    g_ref[...] = jnp.concatenate(gs, axis=0).T
    e_all = jnp.concatenate(es, axis=0).T.astype(jnp.int32)
    e1_ref[...] = lax.shift_right_logical(e_all, nk.bit_length() - 1)
    e2_ref[...] = e_all & (nk - 1)


def _topk(scores, tl=128):
    nhp, nkeys, t = scores.shape
    w = P_HEADS * P_TOPK
    return pl.pallas_call(
        _topk_kernel,
        grid=(t // tl,),
        in_specs=[pl.BlockSpec((nhp, nkeys, tl), lambda i: (0, 0, i))],
        out_specs=[pl.BlockSpec((tl, w), lambda i: (i, 0))] * 3,
        out_shape=[jax.ShapeDtypeStruct((t, w), jnp.int32),
                   jax.ShapeDtypeStruct((t, w), jnp.int32),
                   jax.ShapeDtypeStruct((t, w), F32)],
        compiler_params=_cparams(("arbitrary",)),
        name="topk",
    )(scores)


def _gelu(x):
    return 0.5 * x * (1.0 + lax.erf(x * (2.0 ** -0.5)))


def _peer_kernel(h2_ref, e1_ref, e2_ref, g_ref, wut_ref, wv_ref, x1_ref, gt2_ref, nf_ref, o_ref, gate_scr):
    e = pl.program_id(1)
    tm = h2_ref.shape[0]
    nk = P_NKEYS

    @pl.when(e == 0)
    def _build_gates():
        o_ref[...] = jnp.zeros_like(o_ref)
        sub = lax.broadcasted_iota(jnp.int32, (nk, P_HEADS * P_TOPK), 0)

        def per_token(t, carry):
            a = e1_ref[pl.ds(t, 1), :]
            b = e2_ref[pl.ds(t, 1), :]
            w = g_ref[pl.ds(t, 1), :]
            pt = jnp.where(sub == a, w, 0.0).astype(BF16)
            qt = jnp.where(sub == b, 1.0, 0.0).astype(BF16)
            gate_scr[pl.ds(pl.multiple_of(t * G_PITCH, SUBLANES), nk), :] = _dot_nt(pt, qt)
            return carry

        lax.fori_loop(0, tm, per_token, 0, unroll=SUBLANES)

    h2 = h2_ref[...]
    part = None
    for c in range(PEER_E1 // 2):
        cols = slice(c * 2 * nk, (c + 1) * 2 * nk)
        a = jnp.dot(h2, wut_ref[:, cols], preferred_element_type=F32)
        gates = [gate_scr[pl.ds(e * PEER_E1 + 2 * c + r, tm, stride=G_PITCH), :] for r in range(2)]
        hid = (jnp.concatenate(gates, axis=1) * _gelu(a)).astype(BF16)
        d = jnp.dot(hid, wv_ref[cols, :], preferred_element_type=F32)
        part = d if part is None else part + d
    o_ref[...] += part

    @pl.when(e == pl.num_programs(1) - 1)
    def _finish():
        x2 = x1_ref[...] + gt2_ref[...] * o_ref[...]
        o_ref[...] = _rmsnorm(x2, nf_ref[...])


def _peer(h2, e1, e2, g, wut, wv, x1, mod3, norm_f, row_fn, tm=PEER_TM):
    t = h2.shape[0]
    te = PEER_E1 * P_NKEYS
    ne = wv.shape[0] // te
    once = pl.Buffered(1)
    return pl.pallas_call(
        _peer_kernel,
        grid=(t // tm, ne),
        in_specs=[pl.BlockSpec((tm, D_MODEL), lambda i, e: (i, 0), pipeline_mode=once),
                  pl.BlockSpec((tm, LANES), lambda i, e: (i, 0), pipeline_mode=once),
                  pl.BlockSpec((tm, LANES), lambda i, e: (i, 0), pipeline_mode=once),
                  pl.BlockSpec((tm, LANES), lambda i, e: (i, 0), pipeline_mode=once),
                  pl.BlockSpec((D_MODEL, te), lambda i, e: (0, e)),
                  pl.BlockSpec((te, D_MODEL), lambda i, e: (e, 0)),
                  pl.BlockSpec((tm, D_MODEL), lambda i, e: (i, 0), pipeline_mode=once),
                  _mod_spec(5, row_fn, tm),
                  pl.BlockSpec((1, D_MODEL), lambda i, e: (0, 0))],
        out_specs=pl.BlockSpec((tm, D_MODEL), lambda i, e: (i, 0)),
        out_shape=jax.ShapeDtypeStruct((t, D_MODEL), F32),
        scratch_shapes=[pltpu.VMEM((tm * G_PITCH, LANES), F32)],
        compiler_params=_cparams(("arbitrary", "arbitrary")),
        name="peer",
    )(h2, e1, e2, g, wut, wv, x1, mod3, norm_f)


def _stream(x, mod3, row_fn, wts, s0, ctx_kv, emit_state):
    b, t, d = x.shape
    x2 = x.reshape(b * t, d)
    pa, pb, pg = _inproj(x2, wts["g1"], mod3, wts["w_in"], row_fn)
    pa3 = pa.reshape(b, t, PA_W)
    pb3 = pb.reshape(b, t, PB_W)
    res = _hgrn(pa3, wts["lb_logits"], wts["a_norm"], s0, emit_state)
    oa = res[0].reshape(b * t, A_WIDTH)
    states = res[1] if emit_state else None
    if ctx_kv is None:
        ob = _ctx_attn(pb3, wts["sink"])
    else:
        cos_t, sin_t = _rope_tables(t)
        ob = _lat_attn(pb3, ctx_kv[0], ctx_kv[1], cos_t, sin_t, wts["sink"])
    ob = ob.reshape(b * t, B_WIDTH)
    x1, h2, scores = _postmix(oa, ob, pg, x2, mod3, wts["g2"], wts["w_oa"], wts["w_ob"], wts["w_out"],
                              wts["p_wq"], wts["p_keys"], row_fn)
    e1, e2, g = _topk(scores)
    y = _peer(h2, e1, e2, g, wts["p_u"], wts["p_v"], x1, mod3, wts["norm_f"], row_fn)
    return y.reshape(b, t, d), states, pb3


def kernel(x_prompt, x_sample, state_hgrn, cache_k, cache_v, c, c_ctx, w_ada, b_ada, norm1, norm2,
           w_in, lb_logits, a_norm, b_sink, w_oa, w_ob, w_out, p_wq, p_keys, p_u, p_v, norm_f):
    depth = w_ada.shape[0]
    assert depth == 1, "single-layer step"
    bp, seq, d = x_prompt.shape
    bs, dseq, _ = x_sample.shape
    kvw = B_KV_HEADS * B_HEAD_DIM

    nrow = -(-(bs + 1) // SUBLANES) * SUBLANES
    cond = jnp.zeros((nrow, d), F32).at[:bs].set(c).at[bs].set(c_ctx)
    mod = _ada(cond, w_ada[0], b_ada[0].reshape(1, -1))
    mod3 = mod.reshape(nrow, 1, 6 * d)

    wts = dict(
        g1=norm1[0].reshape(1, d), g2=norm2[0].reshape(1, d), norm_f=norm_f.reshape(1, d),
        w_in=w_in[0].astype(BF16),
        lb_logits=lb_logits.reshape(2, depth + 1, A_HEADS, A_DK).transpose(0, 2, 1, 3),
        a_norm=a_norm[0].reshape(A_HEADS, 1, A_DV),
        sink=b_sink[0],
        w_oa=w_oa[0].astype(BF16), w_ob=w_ob[0].astype(BF16), w_out=w_out[0].astype(BF16),
        p_wq=p_wq[0].astype(BF16),
        p_keys=p_keys[0].reshape(2 * P_HEADS, P_NKEYS, P_QDIM // 2).astype(BF16),
        p_u=p_u[0].astype(BF16).T, p_v=p_v[0].astype(BF16),
    )

    y_prompt, states, pb3 = _stream(x_prompt, mod3, lambda tok: bs, wts, None, None, True)
    ck = cache_k[:, 0].reshape(bs, -1, kvw)
    cv = cache_v[:, 0].reshape(bs, -1, kvw)
    y_sample, _, _ = _stream(x_sample, mod3, lambda tok: tok // dseq, wts, state_hgrn, (ck, cv), False)

    new_k = pb3[:, :, B_WIDTH:B_WIDTH + kvw].reshape(bp, 1, seq, B_KV_HEADS, B_HEAD_DIM)
    new_v = pb3[:, :, B_WIDTH + kvw:].reshape(bp, 1, seq, B_KV_HEADS, B_HEAD_DIM)
    return (y_prompt, y_sample, states.astype(x_prompt.dtype), new_k, new_v)
```

```python
import functools

import jax
import jax.numpy as jnp
import numpy as np
from jax import lax
from jax.experimental import pallas as pl
from jax.experimental.pallas import tpu as pltpu

F32 = jnp.float32
BF16 = jnp.bfloat16
HIGHEST = lax.Precision.HIGHEST
LOG2E = 1.4426950408889634

D_MODEL = 1024
GRID_W = 64
EPS = 1e-6
A_HEADS = 4
A_DK = 128
A_DV = 128
A_WIDTH = A_HEADS * A_DV
A_CHUNK = 16
HGRN_GROUP = 2
B_HEADS = 8
B_KV_HEADS = 2
B_HEAD_DIM = 64
B_WIDTH = B_HEADS * B_HEAD_DIM
WINDOW = 128
B_BLOCK = 128
ROPE_BASE = 10000.0
P_HEADS = 8
P_NKEYS = 128
P_QDIM = 256
P_TOPK = 16

PA_W = 5 * A_WIDTH
PB_W = B_WIDTH + 2 * B_KV_HEADS * B_HEAD_DIM
PG_W = 2 * D_MODEL

LANES = 128
SUBLANES = 8
VMEM_LIMIT = 60 * 1024 * 1024

PEER_TM = 512
PEER_E1 = 4
G_UNROLL = 64
G_PITCH = P_NKEYS + SUBLANES


def _cparams(sem):
    return pltpu.CompilerParams(dimension_semantics=sem, vmem_limit_bytes=VMEM_LIMIT)


def _sigmoid(x):
    return jax.nn.sigmoid(x)


def _silu(x):
    return x * jax.nn.sigmoid(x)


def _rmsnorm(x, g):
    return x * lax.rsqrt(jnp.mean(x * x, axis=-1, keepdims=True) + EPS) * g


def _dot_nt(a, b, **kw):
    return lax.dot_general(a, b, (((1,), (1,)), ((), ())), preferred_element_type=F32, **kw)


def _ada_kernel(c_ref, w_ref, b_ref, o_ref):
    s = _silu(c_ref[...])
    o_ref[...] = jnp.dot(s, w_ref[...], precision=HIGHEST, preferred_element_type=F32) + b_ref[...]


def _ada(cond, w, b):
    n = cond.shape[0]
    nc = w.shape[1] // D_MODEL
    return pl.pallas_call(
        _ada_kernel,
        grid=(nc,),
        in_specs=[pl.BlockSpec((n, D_MODEL), lambda j: (0, 0)),
                  pl.BlockSpec((D_MODEL, D_MODEL), lambda j: (0, j)),
                  pl.BlockSpec((1, D_MODEL), lambda j: (0, j))],
        out_specs=pl.BlockSpec((n, D_MODEL), lambda j: (0, j)),
        out_shape=jax.ShapeDtypeStruct((n, w.shape[1]), F32),
        compiler_params=_cparams(("arbitrary",)),
        name="ada",
    )(cond, w, b)


def _mod_spec(chunk, row_fn, tm):
    return pl.BlockSpec((None, 1, D_MODEL), lambda i, *_: (row_fn(i * tm), 0, chunk))


def _inproj_kernel(x_ref, g_ref, sh_ref, sc_ref, w_ref, pa_ref, pb_ref, pg_ref):
    h = _rmsnorm(x_ref[...], g_ref[...]) * (1.0 + sc_ref[...]) + sh_ref[...]
    hb = h.astype(BF16)
    pa_ref[...] = jnp.dot(hb, w_ref[:, :PA_W], preferred_element_type=F32)
    pb_ref[...] = jnp.dot(hb, w_ref[:, PA_W:PA_W + PB_W], preferred_element_type=F32)
    pg_ref[...] = jnp.dot(hb, w_ref[:, PA_W + PB_W:], preferred_element_type=F32)


def _inproj(x2, g1, mod3, w_in_bf, row_fn, tm=256):
    t = x2.shape[0]
    d_in = w_in_bf.shape[1]
    return pl.pallas_call(
        _inproj_kernel,
        grid=(t // tm,),
        in_specs=[pl.BlockSpec((tm, D_MODEL), lambda i: (i, 0)),
                  pl.BlockSpec((1, D_MODEL), lambda i: (0, 0)),
                  _mod_spec(0, row_fn, tm), _mod_spec(1, row_fn, tm),
                  pl.BlockSpec((D_MODEL, d_in), lambda i: (0, 0))],
        out_specs=[pl.BlockSpec((tm, PA_W), lambda i: (i, 0)),
                   pl.BlockSpec((tm, PB_W), lambda i: (i, 0)),
                   pl.BlockSpec((tm, PG_W), lambda i: (i, 0))],
        out_shape=[jax.ShapeDtypeStruct((t, PA_W), F32),
                   jax.ShapeDtypeStruct((t, PB_W), F32),
                   jax.ShapeDtypeStruct((t, PG_W), F32)],
        compiler_params=_cparams(("arbitrary",)),
        name="inproj",
    )(x2, g1, mod3, mod3, w_in_bf)


def _hgrn_intra(bc2, q, blk_ref, c0, reverse):
    c = A_CHUNK
    half = SUBLANES
    row = lax.broadcasted_iota(jnp.int32, (c, 1), 0)
    acc = [jnp.zeros((half, A_DV), F32), jnp.zeros((half, A_DV), F32)]
    for s in range(c):
        if reverse:
            parts = [0] if s < half else [0, 1]
        else:
            parts = [0, 1] if s < half else [1]
        bcs = blk_ref[0, c0 + s:c0 + s + 1, :]
        ks = blk_ref[1, c0 + s:c0 + s + 1, :]
        vs = blk_ref[2, c0 + s:c0 + s + 1, :]
        for p in parts:
            sl = slice(p * half, (p + 1) * half)
            d = bc2[sl] - bcs
            if (s // half) == p:
                keep = (row[sl] <= s) if reverse else (row[sl] >= s)
                d = jnp.where(keep, d, -jnp.inf)
            w = jnp.exp2(d) * (q[sl] * ks)
            sc = jnp.sum(w, axis=-1, keepdims=True)
            acc[p] = acc[p] + sc * vs
    return jnp.concatenate(acc, axis=0)


def _split3(x):
    hi = x.astype(BF16)
    r1 = x - hi.astype(F32)
    mid = r1.astype(BF16)
    lo = (r1 - mid.astype(F32)).astype(BF16)
    return hi, mid, lo


def _hgrn_block(blk, reverse, st, aq_ref, ai_ref, af_ref, ls, lb, sums_m, blk_ref):
    bt = LANES
    base = pl.multiple_of(blk * bt, bt)
    rows = pl.ds(base, bt)
    q = _silu(aq_ref[rows, ls])
    v = ai_ref[rows, ls]
    f = lb + (1.0 - lb) * _sigmoid(af_ref[rows, ls])
    k = 1.0 - f
    lf2 = jnp.log(f) * LOG2E
    sums = None
    for piece in _split3(lf2):
        d = jnp.dot(sums_m, piece, preferred_element_type=F32)
        sums = d if sums is None else sums + d
    bc2 = sums[:bt]
    rest2 = sums[bt:]
    blk_ref[0] = bc2
    blk_ref[1] = k
    blk_ref[2] = v
    qe = (q * jnp.exp2(bc2)).astype(BF16)
    ke = (k * jnp.exp2(rest2)).astype(BF16)
    ebt = jnp.exp2(bc2 + rest2)
    vt = v.T.astype(BF16)
    tok = lax.broadcasted_iota(jnp.int32, (bt, 1), 0) // A_CHUNK
    nchunk = bt // A_CHUNK
    kem = jnp.concatenate([jnp.where(tok == c, ke, jnp.zeros_like(ke)) for c in range(nchunk)], axis=1)
    ds_all = jnp.dot(vt, kem, preferred_element_type=F32)
    starts = [None] * nchunk
    order = range(nchunk - 1, -1, -1) if reverse else range(nchunk)
    for c in order:
        starts[c] = st.astype(BF16)
        st = st * ebt[c * A_CHUNK:c * A_CHUNK + 1, :] + ds_all[:, c * A_DK:(c + 1) * A_DK]
    outs = []
    for c in range(nchunk):
        cs = slice(c * A_CHUNK, (c + 1) * A_CHUNK)
        o_inter = _dot_nt(qe[cs], starts[c])
        o_intra = _hgrn_intra(bc2[cs], q[cs], blk_ref, c * A_CHUNK, reverse)
        outs.append(o_inter + o_intra)
    return jnp.concatenate(outs, axis=0), st


def _hgrn_kernel(*refs, seq, has_s0, emit_state):
    aq_ref, ai_ref, aff_ref, afb_ref, ag_ref, lbl_ref, gn_ref = refs[:7]
    pos = 7
    s0_ref = None
    if has_s0:
        s0_ref = refs[pos]
        pos += 1
    o_ref = refs[pos]
    pos += 1
    st_ref = None
    if emit_state:
        st_ref = refs[pos]
        pos += 1
    of_scr, ob_scr = refs[pos], refs[pos + 1]
    blk_scr = refs[pos + 2:pos + 2 + 2 * HGRN_GROUP]

    bt = LANES
    nb = seq // bt
    heads = range(HGRN_GROUP)
    lanes = [slice(g * A_DK, (g + 1) * A_DK) for g in heads]

    def lower_bound(g, d):
        z = lbl_ref[d, g]
        m = jnp.max(z, axis=0, keepdims=True)
        e = jnp.exp(z - m)
        return e[0:1, :] / jnp.sum(e, axis=0, keepdims=True)

    lbs = [(lower_bound(g, 0), lower_bound(g, 1)) for g in heads]

    r = lax.broadcasted_iota(jnp.int32, (bt, bt), 0)
    cidx = lax.broadcasted_iota(jnp.int32, (bt, bt), 1)
    same = (r // A_CHUNK) == (cidx // A_CHUNK)
    one = jnp.ones((bt, bt), F32)
    zero = jnp.zeros((bt, bt), F32)

    def sums_matrix(cum_mask, rest_mask):
        return jnp.concatenate([jnp.where(same & cum_mask, one, zero),
                                jnp.where(same & rest_mask, one, zero)], axis=0).astype(BF16)

    sums_f = sums_matrix(cidx <= r, cidx > r)
    sums_b = sums_matrix(cidx >= r, cidx < r)

    init = []
    for g in heads:
        for d in range(2):
            init.append(s0_ref[d, g].T if has_s0 else jnp.zeros((A_DV, A_DK), F32))

    def body(j, carry):
        carry = list(carry)
        jb = nb - 1 - j
        for g in heads:
            o_f, carry[2 * g] = _hgrn_block(j, False, carry[2 * g], aq_ref, ai_ref, aff_ref, lanes[g],
                                            lbs[g][0], sums_f, blk_scr[2 * g])
            of_scr[pl.ds(pl.multiple_of(j * bt, bt), bt), lanes[g]] = o_f
            o_b, carry[2 * g + 1] = _hgrn_block(jb, True, carry[2 * g + 1], aq_ref, ai_ref, afb_ref, lanes[g],
                                                lbs[g][1], sums_b, blk_scr[2 * g + 1])
            ob_scr[pl.ds(pl.multiple_of(jb * bt, bt), bt), lanes[g]] = o_b
        return tuple(carry)

    final = lax.fori_loop(0, nb, body, tuple(init))

    for g in heads:
        o = of_scr[:, lanes[g]] + ob_scr[:, lanes[g]]
        o = o * lax.rsqrt(jnp.mean(o * o, axis=-1, keepdims=True) + EPS) * gn_ref[g]
        o_ref[:, lanes[g]] = (o * _silu(ag_ref[:, lanes[g]])).astype(o_ref.dtype)
        if emit_state:
            st_ref[0, g] = final[2 * g].T
            st_ref[1, g] = final[2 * g + 1].T


def _hgrn(pa3, lb_logits4, a_norm3, s0, emit_state):
    b, t, _ = pa3.shape
    has_s0 = s0 is not None
    hg = HGRN_GROUP
    ngrp = A_HEADS // hg

    def seg(sidx):
        return pl.BlockSpec((None, t, hg * A_DK), lambda bi, h: (bi, 0, sidx * ngrp + h))

    state_spec = pl.BlockSpec((None, None, 2, hg, A_DK, A_DV), lambda bi, h: (bi, 0, 0, h, 0, 0))
    in_specs = [seg(0), seg(1), seg(2), seg(3), seg(4),
                pl.BlockSpec((2, hg, lb_logits4.shape[2], A_DK), lambda bi, h: (0, h, 0, 0)),
                pl.BlockSpec((hg, 1, A_DV), lambda bi, h: (h, 0, 0))]
    args = [pa3, pa3, pa3, pa3, pa3, lb_logits4, a_norm3]
    if has_s0:
        in_specs.append(state_spec)
        args.append(s0)
    out_specs = [pl.BlockSpec((None, t, hg * A_DV), lambda bi, h: (bi, 0, h))]
    out_shape = [jax.ShapeDtypeStruct((b, t, A_WIDTH), BF16)]
    if emit_state:
        out_specs.append(state_spec)
        out_shape.append(jax.ShapeDtypeStruct((b, 1, 2, A_HEADS, A_DK, A_DV), F32))
    return pl.pallas_call(
        functools.partial(_hgrn_kernel, seq=t, has_s0=has_s0, emit_state=emit_state),
        grid=(b, ngrp),
        in_specs=in_specs,
        out_specs=out_specs,
        out_shape=out_shape,
        scratch_shapes=[pltpu.VMEM((t, hg * A_DV), F32), pltpu.VMEM((t, hg * A_DV), F32)]
        + [pltpu.VMEM((3, LANES, A_DK), F32)] * (2 * hg),
        compiler_params=_cparams(("arbitrary", "arbitrary")),
        name="hgrn_state" if emit_state else "hgrn",
    )(*args)


def _dup_half(x, kv, lane):
    swapped = pltpu.roll(x, B_HEAD_DIM, axis=1)
    first = lane < B_HEAD_DIM
    return jnp.where(first, x, swapped) if kv == 0 else jnp.where(first, swapped, x)


def _attend_heads(q_of_pair, key_sets, sink_ref, o_ref, lane):
    scale = B_HEAD_DIM ** -0.5
    group = B_HEADS // B_KV_HEADS
    first = lane < B_HEAD_DIM
    for kv in range(B_KV_HEADS):
        kvs = [(_dup_half(k, kv, lane).astype(BF16), _dup_half(v, kv, lane).astype(BF16), m)
               for k, v, m in key_sets]
        for jj in range(group // 2):
            j = kv * (group // 2) + jj
            qp = q_of_pair(j)
            halves = []
            for half in range(2):
                h = 2 * j + half
                qm = jnp.where(first if half == 0 else jnp.logical_not(first), qp, jnp.zeros_like(qp))
                qm = qm.astype(BF16)
                sink = sink_ref[h]
                ss = []
                for k2, _, m in kvs:
                    s = _dot_nt(qm, k2) * scale
                    if m is not None:
                        s = jnp.where(m, s, -jnp.inf)
                    ss.append(s)
                mx = sink
                for s in ss:
                    mx = jnp.maximum(mx, jnp.max(s, axis=-1, keepdims=True))
                es = [jnp.exp(s - mx) for s in ss]
                den = jnp.exp(sink - mx)
                for e in es:
                    den = den + jnp.sum(e, axis=-1, keepdims=True)
                inv = 1.0 / den
                oh = None
                for e, (_, v2, _) in zip(es, kvs):
                    part = jnp.dot((e * inv).astype(BF16), v2, preferred_element_type=F32)
                    oh = part if oh is None else oh + part
                halves.append(oh)
            o_ref[:, j * LANES:(j + 1) * LANES] = jnp.where(first, halves[0], halves[1]).astype(o_ref.dtype)


def _ctx_attn_kernel(sink_ref, q_ref, kv_ref, o_ref):
    lane = lax.broadcasted_iota(jnp.int32, (1, LANES), 1)
    kvv = kv_ref[...]
    key_sets = [(kvv[:, :LANES], kvv[:, LANES:], None)]
    _attend_heads(lambda j: q_ref[:, j * LANES:(j + 1) * LANES], key_sets, sink_ref, o_ref, lane)


def _ctx_attn(pb3, sink):
    b, t, _ = pb3.shape
    return pl.pallas_call(
        _ctx_attn_kernel,
        grid=(b,),
        in_specs=[pl.BlockSpec(memory_space=pltpu.SMEM),
                  pl.BlockSpec((None, t, B_WIDTH), lambda bi: (bi, 0, 0)),
                  pl.BlockSpec((None, t, 2 * LANES), lambda bi: (bi, 0, B_WIDTH // (2 * LANES)))],
        out_specs=pl.BlockSpec((None, t, B_WIDTH), lambda bi: (bi, 0, 0)),
        out_shape=jax.ShapeDtypeStruct((b, t, B_WIDTH), BF16),
        compiler_params=_cparams(("arbitrary",)),
        name="ctx_attn",
    )(sink, pb3, pb3)


def _rope(x, cos, sin_signed, lane):
    nf = B_HEAD_DIM // 4
    up = pltpu.roll(x, LANES - nf, axis=1)
    down = pltpu.roll(x, nf, axis=1)
    partner = jnp.where((lane % (2 * nf)) < nf, up, down)
    return x * cos + partner * sin_signed


def _lat_attn_kernel(sink_ref, q_ref, kvl_ref, kvc_ref, kvr_ref, ck_ref, cv_ref,
                     cl_ref, cc_ref, cr_ref, sl_ref, sc_ref, sr_ref, o_ref, *, seq):
    n = pl.program_id(1)
    blk = B_BLOCK
    lane = lax.broadcasted_iota(jnp.int32, (1, LANES), 1)
    kw = jnp.concatenate([kvl_ref[:, :LANES], kvc_ref[:, :LANES], kvr_ref[:, :LANES]], axis=0)
    vw = jnp.concatenate([kvl_ref[:, LANES:], kvc_ref[:, LANES:], kvr_ref[:, LANES:]], axis=0)
    cos_w = jnp.concatenate([cl_ref[...], cc_ref[...], cr_ref[...]], axis=0)
    sin_w = jnp.concatenate([sl_ref[...], sc_ref[...], sr_ref[...]], axis=0)
    kw = _rope(kw, cos_w, sin_w, lane)
    qpos = n * blk + lax.broadcasted_iota(jnp.int32, (blk, 3 * blk), 0)
    kpos = (n - 1) * blk + lax.broadcasted_iota(jnp.int32, (blk, 3 * blk), 1)
    mask = (jnp.abs(kpos - qpos) <= WINDOW) & (kpos >= 0) & (kpos < seq)
    key_sets = [(kw, vw, mask), (ck_ref[...], cv_ref[...], None)]
    cos_c = cc_ref[...]
    sin_c = sc_ref[...]

    def q_of_pair(j):
        return _rope(q_ref[:, j * LANES:(j + 1) * LANES], cos_c, sin_c, lane)

    _attend_heads(q_of_pair, key_sets, sink_ref, o_ref, lane)


def _lat_attn(pb3, ck, cv, cos_t, sin_t, sink):
    b, t, _ = pb3.shape
    nb = t // B_BLOCK
    lc = ck.shape[1]
    kv_col = B_WIDTH // (2 * LANES)

    def kv_spec(off):
        return pl.BlockSpec((None, B_BLOCK, 2 * LANES),
                            lambda bi, n: (bi, jnp.clip(n + off, 0, nb - 1), kv_col))

    def tab_spec(off):
        return pl.BlockSpec((B_BLOCK, LANES), lambda bi, n: (jnp.clip(n + off, 0, nb - 1), 0))

    return pl.pallas_call(
        functools.partial(_lat_attn_kernel, seq=t),
        grid=(b, nb),
        in_specs=[pl.BlockSpec(memory_space=pltpu.SMEM),
                  pl.BlockSpec((None, B_BLOCK, B_WIDTH), lambda bi, n: (bi, n, 0)),
                  kv_spec(-1), kv_spec(0), kv_spec(1),
                  pl.BlockSpec((None, lc, LANES), lambda bi, n: (bi, 0, 0)),
                  pl.BlockSpec((None, lc, LANES), lambda bi, n: (bi, 0, 0)),
                  tab_spec(-1), tab_spec(0), tab_spec(1),
                  tab_spec(-1), tab_spec(0), tab_spec(1)],
        out_specs=pl.BlockSpec((None, B_BLOCK, B_WIDTH), lambda bi, n: (bi, n, 0)),
        out_shape=jax.ShapeDtypeStruct((b, t, B_WIDTH), BF16),
        compiler_params=_cparams(("arbitrary", "arbitrary")),
        name="lat_attn",
    )(sink, pb3, pb3, pb3, pb3, ck, cv, cos_t, cos_t, cos_t, sin_t, sin_t, sin_t)


def _rope_tables(seq):
    half = B_HEAD_DIM // 2
    nf = half // 2
    row = jnp.repeat(jnp.arange(seq // GRID_W), GRID_W).astype(F32)
    col = jnp.tile(jnp.arange(GRID_W), seq // GRID_W).astype(F32)
    inv = ROPE_BASE ** (-jnp.arange(nf, dtype=F32) / nf)
    ang_r = row[:, None] * inv[None, :]
    ang_c = col[:, None] * inv[None, :]

    def part(ang):
        c = jnp.cos(ang)
        s = jnp.sin(ang)
        return jnp.concatenate([c, c], axis=-1), jnp.concatenate([-s, s], axis=-1)

    cr, sr = part(ang_r)
    cc, sc = part(ang_c)
    cos_h = jnp.concatenate([cr, cc], axis=-1)
    sin_h = jnp.concatenate([sr, sc], axis=-1)
    reps = LANES // B_HEAD_DIM
    return jnp.tile(cos_h, (1, reps)), jnp.tile(sin_h, (1, reps))


def _postmix_kernel(oa_ref, ob_ref, ga_ref, gb_ref, x_ref, gt1_ref, sh2_ref, sc2_ref, g2_ref,
                    woa_ref, wob_ref, wout_ref, wq_ref, keys_ref, x1_ref, h2_ref, s_ref):
    ya = jnp.dot(oa_ref[...], woa_ref[...], preferred_element_type=F32)
    yb = jnp.dot(ob_ref[...], wob_ref[...], preferred_element_type=F32)
    u = _sigmoid(ga_ref[...]) * ya + _sigmoid(gb_ref[...]) * yb
    y = jnp.dot(u.astype(BF16), wout_ref[...], preferred_element_type=F32)
    x1 = x_ref[...] + gt1_ref[...] * y
    x1_ref[...] = x1
    h2 = _rmsnorm(x1, g2_ref[...]) * (1.0 + sc2_ref[...]) + sh2_ref[...]
    h2b = h2.astype(BF16)
    h2_ref[...] = h2b
    q = jnp.dot(h2b, wq_ref[...], preferred_element_type=F32)
    for hp in range(2 * P_HEADS):
        cs = slice(hp * LANES, (hp + 1) * LANES)
        s_ref[hp] = _dot_nt(keys_ref[hp], q[:, cs].astype(BF16))


def _postmix(oa, ob, pg, x2, mod3, g2, woa, wob, wout, wq, keys, row_fn, tm=256):
    t = x2.shape[0]
    nhp, nkeys = keys.shape[0], keys.shape[1]
    full = lambda shape: pl.BlockSpec(shape, lambda i: (0,) * len(shape))
    return pl.pallas_call(
        _postmix_kernel,
        grid=(t // tm,),
        in_specs=[pl.BlockSpec((tm, A_WIDTH), lambda i: (i, 0)),
                  pl.BlockSpec((tm, B_WIDTH), lambda i: (i, 0)),
                  pl.BlockSpec((tm, D_MODEL), lambda i: (i, 0)),
                  pl.BlockSpec((tm, D_MODEL), lambda i: (i, 1)),
                  pl.BlockSpec((tm, D_MODEL), lambda i: (i, 0)),
                  _mod_spec(2, row_fn, tm), _mod_spec(3, row_fn, tm), _mod_spec(4, row_fn, tm),
                  full((1, D_MODEL)),
                  full(woa.shape), full(wob.shape), full(wout.shape), full(wq.shape), full(keys.shape)],
        out_specs=[pl.BlockSpec((tm, D_MODEL), lambda i: (i, 0)),
                   pl.BlockSpec((tm, D_MODEL), lambda i: (i, 0)),
                   pl.BlockSpec((nhp, nkeys, tm), lambda i: (0, 0, i))],
        out_shape=[jax.ShapeDtypeStruct((t, D_MODEL), F32),
                   jax.ShapeDtypeStruct((t, D_MODEL), BF16),
                   jax.ShapeDtypeStruct((nhp, nkeys, t), F32)],
        compiler_params=_cparams(("arbitrary",)),
        name="postmix",
    )(oa, ob, pg, pg, x2, mod3, mod3, mod3, g2, woa, wob, wout, wq, keys)


def _extract_max(vals, order, big):
    m = jnp.max(vals, axis=0, keepdims=True)
    first = jnp.min(jnp.where(vals == m, order, big), axis=0, keepdims=True)
    return m, first, order == first


def _topk_kernel(s_ref, e1_ref, e2_ref, g_ref):
    tl = s_ref.shape[2]
    nk = P_NKEYS
    kk = P_TOPK
    neg = -jnp.inf
    key_iota = lax.broadcasted_iota(jnp.int32, (nk, tl), 0).astype(F32)
    rank = lax.broadcasted_iota(jnp.int32, (kk, tl), 0)
    gs, es = [], []
    for h in range(P_HEADS):
        tops = []
        for side in range(2):
            s = s_ref[2 * h + side]
            v = jnp.zeros((kk, tl), F32)
            ix = jnp.zeros((kk, tl), F32)
            for a in range(kk):
                m, i, hit = _extract_max(s, key_iota, float(nk))
                s = jnp.where(hit, neg, s)
                v = jnp.where(rank == a, m, v)
                ix = jnp.where(rank == a, i, ix)
            tops.append((v, ix))
        (v1, i1), (v2, i2) = tops
        cands, eids, flats = [], [], []
        for a in range(kk // 2):
            nb = kk // (a + 1)
            nr = -(-nb // SUBLANES) * SUBLANES
            r = lax.broadcasted_iota(jnp.int32, (nr, tl), 0)
            c = v1[a:a + 1] + v2[:nr]
            cands.append(c if nb == nr else jnp.where(r < nb, c, neg))
            eids.append(i1[a:a + 1] * float(nk) + i2[:nr])
            flats.append((r + a * kk).astype(F32))
        r = lax.broadcasted_iota(jnp.int32, (kk // 2, tl), 0)
        cands.append(v1[kk // 2:] + v2[0:1])
        eids.append(i1[kk // 2:] * float(nk) + i2[0:1])
        flats.append(((r + kk // 2) * kk).astype(F32))
        cand = jnp.concatenate(cands, axis=0)
        eid = jnp.concatenate(eids, axis=0)
        flat = jnp.concatenate(flats, axis=0)
        ts = jnp.zeros((kk, tl), F32)
        te = jnp.zeros((kk, tl), F32)
        for k in range(kk):
            m, _, hit = _extract_max(cand, flat, float(kk * kk))
            e = jnp.max(jnp.where(hit, eid, -1.0), axis=0, keepdims=True)
            cand = jnp.where(hit, neg, cand)
            ts = jnp.where(rank == k, m, ts)
            te = jnp.where(rank == k, e, te)
        p = jnp.exp(ts - ts[0:1])
        gs.append(p / jnp.sum(p, axis=0, keepdims=True))
        es.append(te)
    g_ref[...] = jnp.concatenate(gs, axis=0).T
    e_all = jnp.concatenate(es, axis=0).T.astype(jnp.int32)
    e1_ref[...] = lax.shift_right_logical(e_all, nk.bit_length() - 1)
    e2_ref[...] = e_all & (nk - 1)


def _topk(scores, tl=128):
    nhp, nkeys, t = scores.shape
    w = P_HEADS * P_TOPK
    return pl.pallas_call(
        _topk_kernel,
        grid=(t // tl,),
        in_specs=[pl.BlockSpec((nhp, nkeys, tl), lambda i: (0, 0, i))],
        out_specs=[pl.BlockSpec((tl, w), lambda i: (i, 0))] * 3,
        out_shape=[jax.ShapeDtypeStruct((t, w), jnp.int32),
                   jax.ShapeDtypeStruct((t, w), jnp.int32),
                   jax.ShapeDtypeStruct((t, w), F32)],
        compiler_params=_cparams(("arbitrary",)),
        name="topk",
    )(scores)


def _gelu(x):
    return 0.5 * x * (1.0 + lax.erf(x * (2.0 ** -0.5)))


def _peer_kernel(h2_ref, e1_ref, e2_ref, g_ref, wut_ref, wv_ref, x1_ref, gt2_ref, nf_ref, o_ref, gate_scr):
    e = pl.program_id(1)
    tm = h2_ref.shape[0]
    nk = P_NKEYS

    @pl.when(e == 0)
    def _build_gates():
        o_ref[...] = jnp.zeros_like(o_ref)
        sub = lax.broadcasted_iota(jnp.int32, (nk, P_HEADS * P_TOPK), 0)

        def per_token(t, carry):
            a = e1_ref[pl.ds(t, 1), :]
            b = e2_ref[pl.ds(t, 1), :]
            w = g_ref[pl.ds(t, 1), :]
            pt = jnp.where(sub == a, w, 0.0).astype(BF16)
            qt = jnp.where(sub == b, 1.0, 0.0).astype(BF16)
            gate_scr[pl.ds(pl.multiple_of(t * G_PITCH, SUBLANES), nk), :] = _dot_nt(pt, qt)
            return carry

        lax.fori_loop(0, tm, per_token, 0, unroll=G_UNROLL)

    h2 = h2_ref[...]
    part = None
    for c in range(PEER_E1 // 2):
        cols = slice(c * 2 * nk, (c + 1) * 2 * nk)
        a = jnp.dot(h2, wut_ref[:, cols], preferred_element_type=F32)
        gates = [gate_scr[pl.ds(e * PEER_E1 + 2 * c + r, tm, stride=G_PITCH), :] for r in range(2)]
        hid = (jnp.concatenate(gates, axis=1) * _gelu(a)).astype(BF16)
        d = jnp.dot(hid, wv_ref[cols, :], preferred_element_type=F32)
        part = d if part is None else part + d
    o_ref[...] += part

    @pl.when(e == pl.num_programs(1) - 1)
    def _finish():
        x2 = x1_ref[...] + gt2_ref[...] * o_ref[...]
        o_ref[...] = _rmsnorm(x2, nf_ref[...])


def _peer(h2, e1, e2, g, wut, wv, x1, mod3, norm_f, row_fn, tm=PEER_TM):
    t = h2.shape[0]
    te = PEER_E1 * P_NKEYS
    ne = wv.shape[0] // te
    once = pl.Buffered(1)
    return pl.pallas_call(
        _peer_kernel,
        grid=(t // tm, ne),
        in_specs=[pl.BlockSpec((tm, D_MODEL), lambda i, e: (i, 0), pipeline_mode=once),
                  pl.BlockSpec((tm, LANES), lambda i, e: (i, 0), pipeline_mode=once),
                  pl.BlockSpec((tm, LANES), lambda i, e: (i, 0), pipeline_mode=once),
                  pl.BlockSpec((tm, LANES), lambda i, e: (i, 0), pipeline_mode=once),
                  pl.BlockSpec((D_MODEL, te), lambda i, e: (0, e)),
                  pl.BlockSpec((te, D_MODEL), lambda i, e: (e, 0)),
                  pl.BlockSpec((tm, D_MODEL), lambda i, e: (i, 0), pipeline_mode=once),
                  _mod_spec(5, row_fn, tm),
                  pl.BlockSpec((1, D_MODEL), lambda i, e: (0, 0))],
        out_specs=pl.BlockSpec((tm, D_MODEL), lambda i, e: (i, 0)),
        out_shape=jax.ShapeDtypeStruct((t, D_MODEL), F32),
        scratch_shapes=[pltpu.VMEM((tm * G_PITCH, LANES), F32)],
        compiler_params=_cparams(("arbitrary", "arbitrary")),
        name="peer",
    )(h2, e1, e2, g, wut, wv, x1, mod3, norm_f)


def _stream(x, mod3, row_fn, wts, s0, ctx_kv, emit_state):
    b, t, d = x.shape
    x2 = x.reshape(b * t, d)
    pa, pb, pg = _inproj(x2, wts["g1"], mod3, wts["w_in"], row_fn)
    pa3 = pa.reshape(b, t, PA_W)
    pb3 = pb.reshape(b, t, PB_W)
    res = _hgrn(pa3, wts["lb_logits"], wts["a_norm"], s0, emit_state)
    oa = res[0].reshape(b * t, A_WIDTH)
    states = res[1] if emit_state else None
    if ctx_kv is None:
        ob = _ctx_attn(pb3, wts["sink"])
    else:
        cos_t, sin_t = _rope_tables(t)
        ob = _lat_attn(pb3, ctx_kv[0], ctx_kv[1], cos_t, sin_t, wts["sink"])
    ob = ob.reshape(b * t, B_WIDTH)
    x1, h2, scores = _postmix(oa, ob, pg, x2, mod3, wts["g2"], wts["w_oa"], wts["w_ob"], wts["w_out"],
                              wts["p_wq"], wts["p_keys"], row_fn)
    e1, e2, g = _topk(scores)
    y = _peer(h2, e1, e2, g, wts["p_u"], wts["p_v"], x1, mod3, wts["norm_f"], row_fn)
    return y.reshape(b, t, d), states, pb3


def kernel(x_prompt, x_sample, state_hgrn, cache_k, cache_v, c, c_ctx, w_ada, b_ada, norm1, norm2,
           w_in, lb_logits, a_norm, b_sink, w_oa, w_ob, w_out, p_wq, p_keys, p_u, p_v, norm_f):
    depth = w_ada.shape[0]
    assert depth == 1, "single-layer step"
    bp, seq, d = x_prompt.shape
    bs, dseq, _ = x_sample.shape
    kvw = B_KV_HEADS * B_HEAD_DIM

    nrow = -(-(bs + 1) // SUBLANES) * SUBLANES
    cond = jnp.zeros((nrow, d), F32).at[:bs].set(c).at[bs].set(c_ctx)
    mod = _ada(cond, w_ada[0], b_ada[0].reshape(1, -1))
    mod3 = mod.reshape(nrow, 1, 6 * d)

    wts = dict(
        g1=norm1[0].reshape(1, d), g2=norm2[0].reshape(1, d), norm_f=norm_f.reshape(1, d),
        w_in=w_in[0].astype(BF16),
        lb_logits=lb_logits.reshape(2, depth + 1, A_HEADS, A_DK).transpose(0, 2, 1, 3),
        a_norm=a_norm[0].reshape(A_HEADS, 1, A_DV),
        sink=b_sink[0],
        w_oa=w_oa[0].astype(BF16), w_ob=w_ob[0].astype(BF16), w_out=w_out[0].astype(BF16),
        p_wq=p_wq[0].astype(BF16),
        p_keys=p_keys[0].reshape(2 * P_HEADS, P_NKEYS, P_QDIM // 2).astype(BF16),
        p_u=p_u[0].astype(BF16).T, p_v=p_v[0].astype(BF16),
    )

    y_prompt, states, pb3 = _stream(x_prompt, mod3, lambda tok: bs, wts, None, None, True)
    ck = cache_k[:, 0].reshape(bs, -1, kvw)
    cv = cache_v[:, 0].reshape(bs, -1, kvw)
    y_sample, _, _ = _stream(x_sample, mod3, lambda tok: tok // dseq, wts, state_hgrn, (ck, cv), False)

    new_k = pb3[:, :, B_WIDTH:B_WIDTH + kvw].reshape(bp, 1, seq, B_KV_HEADS, B_HEAD_DIM)
    new_v = pb3[:, :, B_WIDTH + kvw:].reshape(bp, 1, seq, B_KV_HEADS, B_HEAD_DIM)
    return (y_prompt, y_sample, states.astype(x_prompt.dtype), new_k, new_v)
```

```python
import functools

import jax
import jax.numpy as jnp
import numpy as np
from jax import lax
from jax.experimental import pallas as pl
from jax.experimental.pallas import tpu as pltpu

F32 = jnp.float32
BF16 = jnp.bfloat16
HIGHEST = lax.Precision.HIGHEST
LOG2E = 1.4426950408889634

D_MODEL = 1024
GRID_W = 64
EPS = 1e-6
A_HEADS = 4
A_DK = 128
A_DV = 128
A_WIDTH = A_HEADS * A_DV
A_CHUNK = 16
HGRN_GROUP = 2
B_HEADS = 8
B_KV_HEADS = 2
B_HEAD_DIM = 64
B_WIDTH = B_HEADS * B_HEAD_DIM
WINDOW = 128
B_BLOCK = 128
ROPE_BASE = 10000.0
P_HEADS = 8
P_NKEYS = 128
P_QDIM = 256
P_TOPK = 16

PA_W = 5 * A_WIDTH
PB_W = B_WIDTH + 2 * B_KV_HEADS * B_HEAD_DIM
PG_W = 2 * D_MODEL

LANES = 128
SUBLANES = 8
VMEM_LIMIT = 60 * 1024 * 1024

PEER_TM = 512
PEER_E1 = 4
G_UNROLL = 64
G_PITCH = P_NKEYS + SUBLANES


def _cparams(sem):
    return pltpu.CompilerParams(dimension_semantics=sem, vmem_limit_bytes=VMEM_LIMIT)


def _sigmoid(x):
    return jax.nn.sigmoid(x)


def _silu(x):
    return x * jax.nn.sigmoid(x)


def _rmsnorm(x, g):
    return x * lax.rsqrt(jnp.mean(x * x, axis=-1, keepdims=True) + EPS) * g


def _dot_nt(a, b, **kw):
    return lax.dot_general(a, b, (((1,), (1,)), ((), ())), preferred_element_type=F32, **kw)


def _ada_kernel(c_ref, w_ref, b_ref, o_ref):
    s = _silu(c_ref[...])
    o_ref[...] = jnp.dot(s, w_ref[...], precision=HIGHEST, preferred_element_type=F32) + b_ref[...]


def _ada(cond, w, b):
    n = cond.shape[0]
    nc = w.shape[1] // D_MODEL
    return pl.pallas_call(
        _ada_kernel,
        grid=(nc,),
        in_specs=[pl.BlockSpec((n, D_MODEL), lambda j: (0, 0)),
                  pl.BlockSpec((D_MODEL, D_MODEL), lambda j: (0, j)),
                  pl.BlockSpec((1, D_MODEL), lambda j: (0, j))],
        out_specs=pl.BlockSpec((n, D_MODEL), lambda j: (0, j)),
        out_shape=jax.ShapeDtypeStruct((n, w.shape[1]), F32),
        compiler_params=_cparams(("arbitrary",)),
        name="ada",
    )(cond, w, b)


def _mod_spec(chunk, row_fn, tm):
    return pl.BlockSpec((None, 1, D_MODEL), lambda i, *_: (row_fn(i * tm), 0, chunk))


def _inproj_kernel(x_ref, g_ref, sh_ref, sc_ref, w_ref, pa_ref, pb_ref, pg_ref):
    h = _rmsnorm(x_ref[...], g_ref[...]) * (1.0 + sc_ref[...]) + sh_ref[...]
    hb = h.astype(BF16)
    pa_ref[...] = jnp.dot(hb, w_ref[:, :PA_W], preferred_element_type=F32)
    pb_ref[...] = jnp.dot(hb, w_ref[:, PA_W:PA_W + PB_W], preferred_element_type=F32)
    pg_ref[...] = jnp.dot(hb, w_ref[:, PA_W + PB_W:], preferred_element_type=F32)


def _inproj(x2, g1, mod3, w_in_bf, row_fn, tm=256):
    t = x2.shape[0]
    d_in = w_in_bf.shape[1]
    return pl.pallas_call(
        _inproj_kernel,
        grid=(t // tm,),
        in_specs=[pl.BlockSpec((tm, D_MODEL), lambda i: (i, 0)),
                  pl.BlockSpec((1, D_MODEL), lambda i: (0, 0)),
                  _mod_spec(0, row_fn, tm), _mod_spec(1, row_fn, tm),
                  pl.BlockSpec((D_MODEL, d_in), lambda i: (0, 0))],
        out_specs=[pl.BlockSpec((tm, PA_W), lambda i: (i, 0)),
                   pl.BlockSpec((tm, PB_W), lambda i: (i, 0)),
                   pl.BlockSpec((tm, PG_W), lambda i: (i, 0))],
        out_shape=[jax.ShapeDtypeStruct((t, PA_W), F32),
                   jax.ShapeDtypeStruct((t, PB_W), F32),
                   jax.ShapeDtypeStruct((t, PG_W), F32)],
        compiler_params=_cparams(("arbitrary",)),
        name="inproj",
    )(x2, g1, mod3, mod3, w_in_bf)


def _hgrn_intra(bc2, q, blk_ref, c0, reverse):
    c = A_CHUNK
    half = SUBLANES
    row = lax.broadcasted_iota(jnp.int32, (c, 1), 0)
    acc = [jnp.zeros((half, A_DV), F32), jnp.zeros((half, A_DV), F32)]
    for s in range(c):
        if reverse:
            parts = [0] if s < half else [0, 1]
        else:
            parts = [0, 1] if s < half else [1]
        bcs = blk_ref[0, c0 + s:c0 + s + 1, :]
        ks = blk_ref[1, c0 + s:c0 + s + 1, :]
        vs = blk_ref[2, c0 + s:c0 + s + 1, :]
        for p in parts:
            sl = slice(p * half, (p + 1) * half)
            d = bc2[sl] - bcs
            if (s // half) == p:
                keep = (row[sl] <= s) if reverse else (row[sl] >= s)
                d = jnp.where(keep, d, -jnp.inf)
            w = jnp.exp2(d) * (q[sl] * ks)
            sc = jnp.sum(w, axis=-1, keepdims=True)
            acc[p] = acc[p] + sc * vs
    return jnp.concatenate(acc, axis=0)


def _split3(x):
    hi = x.astype(BF16)
    r1 = x - hi.astype(F32)
    mid = r1.astype(BF16)
    lo = (r1 - mid.astype(F32)).astype(BF16)
    return hi, mid, lo


def _hgrn_block(blk, reverse, st, aq_ref, ai_ref, af_ref, ls, lb, sums_m, blk_ref):
    bt = LANES
    base = pl.multiple_of(blk * bt, bt)
    rows = pl.ds(base, bt)
    q = _silu(aq_ref[rows, ls])
    v = ai_ref[rows, ls]
    f = lb + (1.0 - lb) * _sigmoid(af_ref[rows, ls])
    k = 1.0 - f
    lf2 = jnp.log(f) * LOG2E
    sums = None
    for piece in _split3(lf2):
        d = jnp.dot(sums_m, piece, preferred_element_type=F32)
        sums = d if sums is None else sums + d
    bc2 = sums[:bt]
    rest2 = sums[bt:]
    blk_ref[0] = bc2
    blk_ref[1] = k
    blk_ref[2] = v
    qe = (q * jnp.exp2(bc2)).astype(BF16)
    ke = (k * jnp.exp2(rest2)).astype(BF16)
    ebt = jnp.exp2(bc2 + rest2)
    vt = v.T.astype(BF16)
    tok = lax.broadcasted_iota(jnp.int32, (bt, 1), 0) // A_CHUNK
    nchunk = bt // A_CHUNK
    kem = jnp.concatenate([jnp.where(tok == c, ke, jnp.zeros_like(ke)) for c in range(nchunk)], axis=1)
    ds_all = jnp.dot(vt, kem, preferred_element_type=F32)
    starts = [None] * nchunk
    order = range(nchunk - 1, -1, -1) if reverse else range(nchunk)
    for c in order:
        starts[c] = st.astype(BF16)
        st = st * ebt[c * A_CHUNK:c * A_CHUNK + 1, :] + ds_all[:, c * A_DK:(c + 1) * A_DK]
    outs = []
    for c in range(nchunk):
        cs = slice(c * A_CHUNK, (c + 1) * A_CHUNK)
        o_inter = _dot_nt(qe[cs], starts[c])
        o_intra = _hgrn_intra(bc2[cs], q[cs], blk_ref, c * A_CHUNK, reverse)
        outs.append(o_inter + o_intra)
    return jnp.concatenate(outs, axis=0), st


def _hgrn_kernel(*refs, seq, has_s0, emit_state):
    aq_ref, ai_ref, aff_ref, afb_ref, ag_ref, lbl_ref, gn_ref = refs[:7]
    pos = 7
    s0_ref = None
    if has_s0:
        s0_ref = refs[pos]
        pos += 1
    o_ref = refs[pos]
    pos += 1
    st_ref = None
    if emit_state:
        st_ref = refs[pos]
        pos += 1
    of_scr, ob_scr = refs[pos], refs[pos + 1]
    blk_scr = refs[pos + 2:pos + 2 + 2 * HGRN_GROUP]

    bt = LANES
    nb = seq // bt
    heads = range(HGRN_GROUP)
    lanes = [slice(g * A_DK, (g + 1) * A_DK) for g in heads]

    def lower_bound(g, d):
        z = lbl_ref[d, g]
        m = jnp.max(z, axis=0, keepdims=True)
        e = jnp.exp(z - m)
        return e[0:1, :] / jnp.sum(e, axis=0, keepdims=True)

    lbs = [(lower_bound(g, 0), lower_bound(g, 1)) for g in heads]

    r = lax.broadcasted_iota(jnp.int32, (bt, bt), 0)
    cidx = lax.broadcasted_iota(jnp.int32, (bt, bt), 1)
    same = (r // A_CHUNK) == (cidx // A_CHUNK)
    one = jnp.ones((bt, bt), F32)
    zero = jnp.zeros((bt, bt), F32)

    def sums_matrix(cum_mask, rest_mask):
        return jnp.concatenate([jnp.where(same & cum_mask, one, zero),
                                jnp.where(same & rest_mask, one, zero)], axis=0).astype(BF16)

    sums_f = sums_matrix(cidx <= r, cidx > r)
    sums_b = sums_matrix(cidx >= r, cidx < r)

    init = []
    for g in heads:
        for d in range(2):
            init.append(s0_ref[d, g].T if has_s0 else jnp.zeros((A_DV, A_DK), F32))

    def body(j, carry):
        carry = list(carry)
        jb = nb - 1 - j
        for g in heads:
            o_f, carry[2 * g] = _hgrn_block(j, False, carry[2 * g], aq_ref, ai_ref, aff_ref, lanes[g],
                                            lbs[g][0], sums_f, blk_scr[2 * g])
            of_scr[pl.ds(pl.multiple_of(j * bt, bt), bt), lanes[g]] = o_f
            o_b, carry[2 * g + 1] = _hgrn_block(jb, True, carry[2 * g + 1], aq_ref, ai_ref, afb_ref, lanes[g],
                                                lbs[g][1], sums_b, blk_scr[2 * g + 1])
            ob_scr[pl.ds(pl.multiple_of(jb * bt, bt), bt), lanes[g]] = o_b
        return tuple(carry)

    final = lax.fori_loop(0, nb, body, tuple(init))

    for g in heads:
        o = of_scr[:, lanes[g]] + ob_scr[:, lanes[g]]
        o = o * lax.rsqrt(jnp.mean(o * o, axis=-1, keepdims=True) + EPS) * gn_ref[g]
        o_ref[:, lanes[g]] = (o * _silu(ag_ref[:, lanes[g]])).astype(o_ref.dtype)
        if emit_state:
            st_ref[0, g] = final[2 * g].T
            st_ref[1, g] = final[2 * g + 1].T


def _hgrn(pa3, lb_logits4, a_norm3, s0, emit_state):
    b, t, _ = pa3.shape
    has_s0 = s0 is not None
    hg = HGRN_GROUP
    ngrp = A_HEADS // hg

    def seg(sidx):
        return pl.BlockSpec((None, t, hg * A_DK), lambda bi, h: (bi, 0, sidx * ngrp + h))

    state_spec = pl.BlockSpec((None, None, 2, hg, A_DK, A_DV), lambda bi, h: (bi, 0, 0, h, 0, 0))
    in_specs = [seg(0), seg(1), seg(2), seg(3), seg(4),
                pl.BlockSpec((2, hg, lb_logits4.shape[2], A_DK), lambda bi, h: (0, h, 0, 0)),
                pl.BlockSpec((hg, 1, A_DV), lambda bi, h: (h, 0, 0))]
    args = [pa3, pa3, pa3, pa3, pa3, lb_logits4, a_norm3]
    if has_s0:
        in_specs.append(state_spec)
        args.append(s0)
    out_specs = [pl.BlockSpec((None, t, hg * A_DV), lambda bi, h: (bi, 0, h))]
    out_shape = [jax.ShapeDtypeStruct((b, t, A_WIDTH), BF16)]
    if emit_state:
        out_specs.append(state_spec)
        out_shape.append(jax.ShapeDtypeStruct((b, 1, 2, A_HEADS, A_DK, A_DV), F32))
    return pl.pallas_call(
        functools.partial(_hgrn_kernel, seq=t, has_s0=has_s0, emit_state=emit_state),
        grid=(b, ngrp),
        in_specs=in_specs,
        out_specs=out_specs,
        out_shape=out_shape,
        scratch_shapes=[pltpu.VMEM((t, hg * A_DV), F32), pltpu.VMEM((t, hg * A_DV), F32)]
        + [pltpu.VMEM((3, LANES, A_DK), F32)] * (2 * hg),
        compiler_params=_cparams(("arbitrary", "arbitrary")),
        name="hgrn_state" if emit_state else "hgrn",
    )(*args)


def _dup_half(x, kv, lane):
    swapped = pltpu.roll(x, B_HEAD_DIM, axis=1)
    first = lane < B_HEAD_DIM
    return jnp.where(first, x, swapped) if kv == 0 else jnp.where(first, swapped, x)


def _attend_heads(q_of_pair, key_sets, sink_ref, o_ref, lane):
    scale = B_HEAD_DIM ** -0.5
    group = B_HEADS // B_KV_HEADS
    first = lane < B_HEAD_DIM
    lq = o_ref.shape[0]
    for kv in range(B_KV_HEADS):
        kvs = [(_dup_half(k, kv, lane).astype(BF16), _dup_half(v, kv, lane).astype(BF16), m)
               for k, v, m in key_sets]
        qs, sinks = [], []
        for jj in range(group // 2):
            j = kv * (group // 2) + jj
            qp = q_of_pair(j)
            for half in range(2):
                keep = first if half == 0 else jnp.logical_not(first)
                qs.append(jnp.where(keep, qp, jnp.zeros_like(qp)).astype(BF16))
                sinks.append(jnp.full((lq, 1), sink_ref[2 * j + half], F32))
        qs = jnp.concatenate(qs, axis=0)
        sink = jnp.concatenate(sinks, axis=0)
        ss = []
        for k2, _, m in kvs:
            s = _dot_nt(qs, k2) * scale
            if m is not None:
                s = jnp.where(jnp.concatenate([m] * group, axis=0), s, -jnp.inf)
            ss.append(s)
        mx = sink
        for s in ss:
            mx = jnp.maximum(mx, jnp.max(s, axis=-1, keepdims=True))
        es = [jnp.exp(s - mx) for s in ss]
        den = jnp.exp(sink - mx)
        for e in es:
            den = den + jnp.sum(e, axis=-1, keepdims=True)
        inv = 1.0 / den
        og = None
        for e, (_, v2, _) in zip(es, kvs):
            part = jnp.dot((e * inv).astype(BF16), v2, preferred_element_type=F32)
            og = part if og is None else og + part
        for jj in range(group // 2):
            j = kv * (group // 2) + jj
            even = og[(2 * jj) * lq:(2 * jj + 1) * lq]
            odd = og[(2 * jj + 1) * lq:(2 * jj + 2) * lq]
            o_ref[:, j * LANES:(j + 1) * LANES] = jnp.where(first, even, odd).astype(o_ref.dtype)


def _ctx_attn_kernel(sink_ref, q_ref, kv_ref, o_ref):
    lane = lax.broadcasted_iota(jnp.int32, (1, LANES), 1)
    kvv = kv_ref[...]
    key_sets = [(kvv[:, :LANES], kvv[:, LANES:], None)]
    _attend_heads(lambda j: q_ref[:, j * LANES:(j + 1) * LANES], key_sets, sink_ref, o_ref, lane)


def _ctx_attn(pb3, sink):
    b, t, _ = pb3.shape
    return pl.pallas_call(
        _ctx_attn_kernel,
        grid=(b,),
        in_specs=[pl.BlockSpec(memory_space=pltpu.SMEM),
                  pl.BlockSpec((None, t, B_WIDTH), lambda bi: (bi, 0, 0)),
                  pl.BlockSpec((None, t, 2 * LANES), lambda bi: (bi, 0, B_WIDTH // (2 * LANES)))],
        out_specs=pl.BlockSpec((None, t, B_WIDTH), lambda bi: (bi, 0, 0)),
        out_shape=jax.ShapeDtypeStruct((b, t, B_WIDTH), BF16),
        compiler_params=_cparams(("arbitrary",)),
        name="ctx_attn",
    )(sink, pb3, pb3)


def _rope(x, cos, sin_signed, lane):
    nf = B_HEAD_DIM // 4
    up = pltpu.roll(x, LANES - nf, axis=1)
    down = pltpu.roll(x, nf, axis=1)
    partner = jnp.where((lane % (2 * nf)) < nf, up, down)
    return x * cos + partner * sin_signed


def _lat_attn_kernel(sink_ref, q_ref, kvl_ref, kvc_ref, kvr_ref, ck_ref, cv_ref,
                     cl_ref, cc_ref, cr_ref, sl_ref, sc_ref, sr_ref, o_ref, *, seq):
    n = pl.program_id(1)
    blk = B_BLOCK
    lane = lax.broadcasted_iota(jnp.int32, (1, LANES), 1)
    kw = jnp.concatenate([kvl_ref[:, :LANES], kvc_ref[:, :LANES], kvr_ref[:, :LANES]], axis=0)
    vw = jnp.concatenate([kvl_ref[:, LANES:], kvc_ref[:, LANES:], kvr_ref[:, LANES:]], axis=0)
    cos_w = jnp.concatenate([cl_ref[...], cc_ref[...], cr_ref[...]], axis=0)
    sin_w = jnp.concatenate([sl_ref[...], sc_ref[...], sr_ref[...]], axis=0)
    kw = _rope(kw, cos_w, sin_w, lane)
    qpos = n * blk + lax.broadcasted_iota(jnp.int32, (blk, 3 * blk), 0)
    kpos = (n - 1) * blk + lax.broadcasted_iota(jnp.int32, (blk, 3 * blk), 1)
    mask = (jnp.abs(kpos - qpos) <= WINDOW) & (kpos >= 0) & (kpos < seq)
    key_sets = [(kw, vw, mask), (ck_ref[...], cv_ref[...], None)]
    cos_c = cc_ref[...]
    sin_c = sc_ref[...]

    def q_of_pair(j):
        return _rope(q_ref[:, j * LANES:(j + 1) * LANES], cos_c, sin_c, lane)

    _attend_heads(q_of_pair, key_sets, sink_ref, o_ref, lane)


def _lat_attn(pb3, ck, cv, cos_t, sin_t, sink):
    b, t, _ = pb3.shape
    nb = t // B_BLOCK
    lc = ck.shape[1]
    kv_col = B_WIDTH // (2 * LANES)

    def kv_spec(off):
        return pl.BlockSpec((None, B_BLOCK, 2 * LANES),
                            lambda bi, n: (bi, jnp.clip(n + off, 0, nb - 1), kv_col))

    def tab_spec(off):
        return pl.BlockSpec((B_BLOCK, LANES), lambda bi, n: (jnp.clip(n + off, 0, nb - 1), 0))

    return pl.pallas_call(
        functools.partial(_lat_attn_kernel, seq=t),
        grid=(b, nb),
        in_specs=[pl.BlockSpec(memory_space=pltpu.SMEM),
                  pl.BlockSpec((None, B_BLOCK, B_WIDTH), lambda bi, n: (bi, n, 0)),
                  kv_spec(-1), kv_spec(0), kv_spec(1),
                  pl.BlockSpec((None, lc, LANES), lambda bi, n: (bi, 0, 0)),
                  pl.BlockSpec((None, lc, LANES), lambda bi, n: (bi, 0, 0)),
                  tab_spec(-1), tab_spec(0), tab_spec(1),
                  tab_spec(-1), tab_spec(0), tab_spec(1)],
        out_specs=pl.BlockSpec((None, B_BLOCK, B_WIDTH), lambda bi, n: (bi, n, 0)),
        out_shape=jax.ShapeDtypeStruct((b, t, B_WIDTH), BF16),
        compiler_params=_cparams(("arbitrary", "arbitrary")),
        name="lat_attn",
    )(sink, pb3, pb3, pb3, pb3, ck, cv, cos_t, cos_t, cos_t, sin_t, sin_t, sin_t)


def _rope_tables(seq):
    half = B_HEAD_DIM // 2
    nf = half // 2
    row = jnp.repeat(jnp.arange(seq // GRID_W), GRID_W).astype(F32)
    col = jnp.tile(jnp.arange(GRID_W), seq // GRID_W).astype(F32)
    inv = ROPE_BASE ** (-jnp.arange(nf, dtype=F32) / nf)
    ang_r = row[:, None] * inv[None, :]
    ang_c = col[:, None] * inv[None, :]

    def part(ang):
        c = jnp.cos(ang)
        s = jnp.sin(ang)
        return jnp.concatenate([c, c], axis=-1), jnp.concatenate([-s, s], axis=-1)

    cr, sr = part(ang_r)
    cc, sc = part(ang_c)
    cos_h = jnp.concatenate([cr, cc], axis=-1)
    sin_h = jnp.concatenate([sr, sc], axis=-1)
    reps = LANES // B_HEAD_DIM
    return jnp.tile(cos_h, (1, reps)), jnp.tile(sin_h, (1, reps))


def _postmix_kernel(oa_ref, ob_ref, ga_ref, gb_ref, x_ref, gt1_ref, sh2_ref, sc2_ref, g2_ref,
                    woa_ref, wob_ref, wout_ref, wq_ref, keys_ref, x1_ref, h2_ref, s_ref):
    ya = jnp.dot(oa_ref[...], woa_ref[...], preferred_element_type=F32)
    yb = jnp.dot(ob_ref[...], wob_ref[...], preferred_element_type=F32)
    u = _sigmoid(ga_ref[...]) * ya + _sigmoid(gb_ref[...]) * yb
    y = jnp.dot(u.astype(BF16), wout_ref[...], preferred_element_type=F32)
    x1 = x_ref[...] + gt1_ref[...] * y
    x1_ref[...] = x1
    h2 = _rmsnorm(x1, g2_ref[...]) * (1.0 + sc2_ref[...]) + sh2_ref[...]
    h2b = h2.astype(BF16)
    h2_ref[...] = h2b
    q = jnp.dot(h2b, wq_ref[...], preferred_element_type=F32)
    for hp in range(2 * P_HEADS):
        cs = slice(hp * LANES, (hp + 1) * LANES)
        s_ref[hp] = _dot_nt(keys_ref[hp], q[:, cs].astype(BF16))


def _postmix(oa, ob, pg, x2, mod3, g2, woa, wob, wout, wq, keys, row_fn, tm=256):
    t = x2.shape[0]
    nhp, nkeys = keys.shape[0], keys.shape[1]
    full = lambda shape: pl.BlockSpec(shape, lambda i: (0,) * len(shape))
    return pl.pallas_call(
        _postmix_kernel,
        grid=(t // tm,),
        in_specs=[pl.BlockSpec((tm, A_WIDTH), lambda i: (i, 0)),
                  pl.BlockSpec((tm, B_WIDTH), lambda i: (i, 0)),
                  pl.BlockSpec((tm, D_MODEL), lambda i: (i, 0)),
                  pl.BlockSpec((tm, D_MODEL), lambda i: (i, 1)),
                  pl.BlockSpec((tm, D_MODEL), lambda i: (i, 0)),
                  _mod_spec(2, row_fn, tm), _mod_spec(3, row_fn, tm), _mod_spec(4, row_fn, tm),
                  full((1, D_MODEL)),
                  full(woa.shape), full(wob.shape), full(wout.shape), full(wq.shape), full(keys.shape)],
        out_specs=[pl.BlockSpec((tm, D_MODEL), lambda i: (i, 0)),
                   pl.BlockSpec((tm, D_MODEL), lambda i: (i, 0)),
                   pl.BlockSpec((nhp, nkeys, tm), lambda i: (0, 0, i))],
        out_shape=[jax.ShapeDtypeStruct((t, D_MODEL), F32),
                   jax.ShapeDtypeStruct((t, D_MODEL), BF16),
                   jax.ShapeDtypeStruct((nhp, nkeys, t), F32)],
        compiler_params=_cparams(("arbitrary",)),
        name="postmix",
    )(oa, ob, pg, pg, x2, mod3, mod3, mod3, g2, woa, wob, wout, wq, keys)


def _extract_max(vals, order, big):
    m = jnp.max(vals, axis=0, keepdims=True)
    first = jnp.min(jnp.where(vals == m, order, big), axis=0, keepdims=True)
    return m, first, order == first


def _topk_kernel(s_ref, e1_ref, e2_ref, g_ref):
    tl = s_ref.shape[2]
    nk = P_NKEYS
    kk = P_TOPK
    neg = -jnp.inf
    key_iota = lax.broadcasted_iota(jnp.int32, (nk, tl), 0).astype(F32)
    rank = lax.broadcasted_iota(jnp.int32, (kk, tl), 0)
    gs, es = [], []
    for h in range(P_HEADS):
        tops = []
        for side in range(2):
            s = s_ref[2 * h + side]
            v = jnp.zeros((kk, tl), F32)
            ix = jnp.zeros((kk, tl), F32)
            for a in range(kk):
                m, i, hit = _extract_max(s, key_iota, float(nk))
                s = jnp.where(hit, neg, s)
                v = jnp.where(rank == a, m, v)
                ix = jnp.where(rank == a, i, ix)
            tops.append((v, ix))
        (v1, i1), (v2, i2) = tops
        cands, eids, flats = [], [], []
        for a in range(kk // 2):
            nb = kk // (a + 1)
            nr = -(-nb // SUBLANES) * SUBLANES
            r = lax.broadcasted_iota(jnp.int32, (nr, tl), 0)
            c = v1[a:a + 1] + v2[:nr]
            cands.append(c if nb == nr else jnp.where(r < nb, c, neg))
            eids.append(i1[a:a + 1] * float(nk) + i2[:nr])
            flats.append((r + a * kk).astype(F32))
        r = lax.broadcasted_iota(jnp.int32, (kk // 2, tl), 0)
        cands.append(v1[kk // 2:] + v2[0:1])
        eids.append(i1[kk // 2:] * float(nk) + i2[0:1])
        flats.append(((r + kk // 2) * kk).astype(F32))
        cand = jnp.concatenate(cands, axis=0)
        eid = jnp.concatenate(eids, axis=0)
        flat = jnp.concatenate(flats, axis=0)
        ts = jnp.zeros((kk, tl), F32)
        te = jnp.zeros((kk, tl), F32)
        nexp = float(nk * nk)
        okey = flat * nexp + eid
        for k in range(kk):
            m, first, hit = _extract_max(cand, okey, float(kk * kk) * nexp)
            e = first - jnp.floor(first * (1.0 / nexp)) * nexp
            cand = jnp.where(hit, neg, cand)
            ts = jnp.where(rank == k, m, ts)
            te = jnp.where(rank == k, e, te)
        p = jnp.exp(ts - ts[0:1])
        gs.append(p / jnp.sum(p, axis=0, keepdims=True))
        es.append(te)
    g_ref[...] = jnp.concatenate(gs, axis=0).T
    e_all = jnp.concatenate(es, axis=0).T.astype(jnp.int32)
    e1_ref[...] = lax.shift_right_logical(e_all, nk.bit_length() - 1)
    e2_ref[...] = e_all & (nk - 1)


def _topk(scores, tl=128):
    nhp, nkeys, t = scores.shape
    w = P_HEADS * P_TOPK
    return pl.pallas_call(
        _topk_kernel,
        grid=(t // tl,),
        in_specs=[pl.BlockSpec((nhp, nkeys, tl), lambda i: (0, 0, i))],
        out_specs=[pl.BlockSpec((tl, w), lambda i: (i, 0))] * 3,
        out_shape=[jax.ShapeDtypeStruct((t, w), jnp.int32),
                   jax.ShapeDtypeStruct((t, w), jnp.int32),
                   jax.ShapeDtypeStruct((t, w), F32)],
        compiler_params=_cparams(("arbitrary",)),
        name="topk",
    )(scores)


def _gelu(x):
    return 0.5 * x * (1.0 + lax.erf(x * (2.0 ** -0.5)))


def _peer_kernel(h2_ref, e1_ref, e2_ref, g_ref, wut_ref, wv_ref, x1_ref, gt2_ref, nf_ref, o_ref, gate_scr):
    e = pl.program_id(1)
    tm = h2_ref.shape[0]
    nk = P_NKEYS

    @pl.when(e == 0)
    def _build_gates():
        o_ref[...] = jnp.zeros_like(o_ref)
        sub = lax.broadcasted_iota(jnp.int32, (nk, P_HEADS * P_TOPK), 0)

        def per_token(t, carry):
            a = e1_ref[pl.ds(t, 1), :]
            b = e2_ref[pl.ds(t, 1), :]
            w = g_ref[pl.ds(t, 1), :]
            pt = jnp.where(sub == a, w, 0.0).astype(BF16)
            qt = jnp.where(sub == b, 1.0, 0.0).astype(BF16)
            gate_scr[pl.ds(pl.multiple_of(t * G_PITCH, SUBLANES), nk), :] = _dot_nt(pt, qt)
            return carry

        lax.fori_loop(0, tm, per_token, 0, unroll=G_UNROLL)

    h2 = h2_ref[...]
    hids = []
    for c in range(PEER_E1 // 2):
        cols = slice(c * 2 * nk, (c + 1) * 2 * nk)
        a = jnp.dot(h2, wut_ref[:, cols], preferred_element_type=F32)
        gates = [gate_scr[pl.ds(e * PEER_E1 + 2 * c + r, tm, stride=G_PITCH), :] for r in range(2)]
        hids.append((jnp.concatenate(gates, axis=1) * _gelu(a)).astype(BF16))
    o_ref[...] += jnp.dot(jnp.concatenate(hids, axis=1), wv_ref[...], preferred_element_type=F32)

    @pl.when(e == pl.num_programs(1) - 1)
    def _finish():
        x2 = x1_ref[...] + gt2_ref[...] * o_ref[...]
        o_ref[...] = _rmsnorm(x2, nf_ref[...])


def _peer(h2, e1, e2, g, wut, wv, x1, mod3, norm_f, row_fn, tm=PEER_TM):
    t = h2.shape[0]
    te = PEER_E1 * P_NKEYS
    ne = wv.shape[0] // te
    once = pl.Buffered(1)
    return pl.pallas_call(
        _peer_kernel,
        grid=(t // tm, ne),
        in_specs=[pl.BlockSpec((tm, D_MODEL), lambda i, e: (i, 0), pipeline_mode=once),
                  pl.BlockSpec((tm, LANES), lambda i, e: (i, 0), pipeline_mode=once),
                  pl.BlockSpec((tm, LANES), lambda i, e: (i, 0), pipeline_mode=once),
                  pl.BlockSpec((tm, LANES), lambda i, e: (i, 0), pipeline_mode=once),
                  pl.BlockSpec((D_MODEL, te), lambda i, e: (0, e)),
                  pl.BlockSpec((te, D_MODEL), lambda i, e: (e, 0)),
                  pl.BlockSpec((tm, D_MODEL), lambda i, e: (i, 0), pipeline_mode=once),
                  _mod_spec(5, row_fn, tm),
                  pl.BlockSpec((1, D_MODEL), lambda i, e: (0, 0))],
        out_specs=pl.BlockSpec((tm, D_MODEL), lambda i, e: (i, 0)),
        out_shape=jax.ShapeDtypeStruct((t, D_MODEL), F32),
        scratch_shapes=[pltpu.VMEM((tm * G_PITCH, LANES), F32)],
        compiler_params=_cparams(("arbitrary", "arbitrary")),
        name="peer",
    )(h2, e1, e2, g, wut, wv, x1, mod3, norm_f)


def _stream(x, mod3, row_fn, wts, s0, ctx_kv, emit_state):
    b, t, d = x.shape
    x2 = x.reshape(b * t, d)
    pa, pb, pg = _inproj(x2, wts["g1"], mod3, wts["w_in"], row_fn)
    pa3 = pa.reshape(b, t, PA_W)
    pb3 = pb.reshape(b, t, PB_W)
    res = _hgrn(pa3, wts["lb_logits"], wts["a_norm"], s0, emit_state)
    oa = res[0].reshape(b * t, A_WIDTH)
    states = res[1] if emit_state else None
    if ctx_kv is None:
        ob = _ctx_attn(pb3, wts["sink"])
    else:
        cos_t, sin_t = _rope_tables(t)
        ob = _lat_attn(pb3, ctx_kv[0], ctx_kv[1], cos_t, sin_t, wts["sink"])
    ob = ob.reshape(b * t, B_WIDTH)
    x1, h2, scores = _postmix(oa, ob, pg, x2, mod3, wts["g2"], wts["w_oa"], wts["w_ob"], wts["w_out"],
                              wts["p_wq"], wts["p_keys"], row_fn)
    e1, e2, g = _topk(scores)
    y = _peer(h2, e1, e2, g, wts["p_u"], wts["p_v"], x1, mod3, wts["norm_f"], row_fn)
    return y.reshape(b, t, d), states, pb3


def kernel(x_prompt, x_sample, state_hgrn, cache_k, cache_v, c, c_ctx, w_ada, b_ada, norm1, norm2,
           w_in, lb_logits, a_norm, b_sink, w_oa, w_ob, w_out, p_wq, p_keys, p_u, p_v, norm_f):
    depth = w_ada.shape[0]
    assert depth == 1, "single-layer step"
    bp, seq, d = x_prompt.shape
    bs, dseq, _ = x_sample.shape
    kvw = B_KV_HEADS * B_HEAD_DIM

    nrow = -(-(bs + 1) // SUBLANES) * SUBLANES
    cond = jnp.zeros((nrow, d), F32).at[:bs].set(c).at[bs].set(c_ctx)
    mod = _ada(cond, w_ada[0], b_ada[0].reshape(1, -1))
    mod3 = mod.reshape(nrow, 1, 6 * d)

    wts = dict(
        g1=norm1[0].reshape(1, d), g2=norm2[0].reshape(1, d), norm_f=norm_f.reshape(1, d),
        w_in=w_in[0].astype(BF16),
        lb_logits=lb_logits.reshape(2, depth + 1, A_HEADS, A_DK).transpose(0, 2, 1, 3),
        a_norm=a_norm[0].reshape(A_HEADS, 1, A_DV),
        sink=b_sink[0],
        w_oa=w_oa[0].astype(BF16), w_ob=w_ob[0].astype(BF16), w_out=w_out[0].astype(BF16),
        p_wq=p_wq[0].astype(BF16),
        p_keys=p_keys[0].reshape(2 * P_HEADS, P_NKEYS, P_QDIM // 2).astype(BF16),
        p_u=p_u[0].astype(BF16).T, p_v=p_v[0].astype(BF16),
    )

    y_prompt, states, pb3 = _stream(x_prompt, mod3, lambda tok: bs, wts, None, None, True)
    ck = cache_k[:, 0].reshape(bs, -1, kvw)
    cv = cache_v[:, 0].reshape(bs, -1, kvw)
    y_sample, _, _ = _stream(x_sample, mod3, lambda tok: tok // dseq, wts, state_hgrn, (ck, cv), False)

    new_k = pb3[:, :, B_WIDTH:B_WIDTH + kvw].reshape(bp, 1, seq, B_KV_HEADS, B_HEAD_DIM)
    new_v = pb3[:, :, B_WIDTH + kvw:].reshape(bp, 1, seq, B_KV_HEADS, B_HEAD_DIM)
    return (y_prompt, y_sample, states.astype(x_prompt.dtype), new_k, new_v)
```

```python
import functools

import jax
import jax.numpy as jnp
import numpy as np
from jax import lax
from jax.experimental import pallas as pl
from jax.experimental.pallas import tpu as pltpu

F32 = jnp.float32
BF16 = jnp.bfloat16
HIGHEST = lax.Precision.HIGHEST
LOG2E = 1.4426950408889634

D_MODEL = 1024
GRID_W = 64
EPS = 1e-6
A_HEADS = 4
A_DK = 128
A_DV = 128
A_WIDTH = A_HEADS * A_DV
A_CHUNK = 16
HGRN_GROUP = 2
B_HEADS = 8
B_KV_HEADS = 2
B_HEAD_DIM = 64
B_WIDTH = B_HEADS * B_HEAD_DIM
WINDOW = 128
B_BLOCK = 128
ROPE_BASE = 10000.0
P_HEADS = 8
P_NKEYS = 128
P_QDIM = 256
P_TOPK = 16

PA_W = 5 * A_WIDTH
PB_W = B_WIDTH + 2 * B_KV_HEADS * B_HEAD_DIM
PG_W = 2 * D_MODEL

LANES = 128
SUBLANES = 8
VMEM_LIMIT = 60 * 1024 * 1024

PEER_TM = 512
PEER_E1 = 4
G_UNROLL = 64
G_PITCH = P_NKEYS + SUBLANES


def _cparams(sem):
    return pltpu.CompilerParams(dimension_semantics=sem, vmem_limit_bytes=VMEM_LIMIT)


def _sigmoid(x):
    return jax.nn.sigmoid(x)


def _silu(x):
    return x * jax.nn.sigmoid(x)


def _rmsnorm(x, g):
    return x * lax.rsqrt(jnp.mean(x * x, axis=-1, keepdims=True) + EPS) * g


def _dot_nt(a, b, **kw):
    return lax.dot_general(a, b, (((1,), (1,)), ((), ())), preferred_element_type=F32, **kw)


def _ada_kernel(c_ref, w_ref, b_ref, o_ref):
    s = _silu(c_ref[...])
    o_ref[...] = jnp.dot(s, w_ref[...], precision=HIGHEST, preferred_element_type=F32) + b_ref[...]


def _ada(cond, w, b):
    n = cond.shape[0]
    nc = w.shape[1] // D_MODEL
    return pl.pallas_call(
        _ada_kernel,
        grid=(nc,),
        in_specs=[pl.BlockSpec((n, D_MODEL), lambda j: (0, 0)),
                  pl.BlockSpec((D_MODEL, D_MODEL), lambda j: (0, j)),
                  pl.BlockSpec((1, D_MODEL), lambda j: (0, j))],
        out_specs=pl.BlockSpec((n, D_MODEL), lambda j: (0, j)),
        out_shape=jax.ShapeDtypeStruct((n, w.shape[1]), F32),
        compiler_params=_cparams(("arbitrary",)),
        name="ada",
    )(cond, w, b)


def _mod_spec(chunk, row_fn, tm):
    return pl.BlockSpec((None, 1, D_MODEL), lambda i, *_: (row_fn(i * tm), 0, chunk))


def _inproj_kernel(x_ref, g_ref, sh_ref, sc_ref, w_ref, pa_ref, pb_ref, pg_ref):
    h = _rmsnorm(x_ref[...], g_ref[...]) * (1.0 + sc_ref[...]) + sh_ref[...]
    hb = h.astype(BF16)
    pa_ref[...] = jnp.dot(hb, w_ref[:, :PA_W], preferred_element_type=F32)
    pb_ref[...] = jnp.dot(hb, w_ref[:, PA_W:PA_W + PB_W], preferred_element_type=F32)
    pg_ref[...] = jnp.dot(hb, w_ref[:, PA_W + PB_W:], preferred_element_type=F32)


def _inproj(x2, g1, mod3, w_in_bf, row_fn, tm=256):
    t = x2.shape[0]
    d_in = w_in_bf.shape[1]
    return pl.pallas_call(
        _inproj_kernel,
        grid=(t // tm,),
        in_specs=[pl.BlockSpec((tm, D_MODEL), lambda i: (i, 0)),
                  pl.BlockSpec((1, D_MODEL), lambda i: (0, 0)),
                  _mod_spec(0, row_fn, tm), _mod_spec(1, row_fn, tm),
                  pl.BlockSpec((D_MODEL, d_in), lambda i: (0, 0))],
        out_specs=[pl.BlockSpec((tm, PA_W), lambda i: (i, 0)),
                   pl.BlockSpec((tm, PB_W), lambda i: (i, 0)),
                   pl.BlockSpec((tm, PG_W), lambda i: (i, 0))],
        out_shape=[jax.ShapeDtypeStruct((t, PA_W), F32),
                   jax.ShapeDtypeStruct((t, PB_W), F32),
                   jax.ShapeDtypeStruct((t, PG_W), F32)],
        compiler_params=_cparams(("arbitrary",)),
        name="inproj",
    )(x2, g1, mod3, mod3, w_in_bf)


def _hgrn_intra(bc2, q, blk_ref, c0, reverse):
    c = A_CHUNK
    half = SUBLANES
    row = lax.broadcasted_iota(jnp.int32, (c, 1), 0)
    acc = [jnp.zeros((half, A_DV), F32), jnp.zeros((half, A_DV), F32)]
    for s in range(c):
        if reverse:
            parts = [0] if s < half else [0, 1]
        else:
            parts = [0, 1] if s < half else [1]
        bcs = blk_ref[0, c0 + s:c0 + s + 1, :]
        ks = blk_ref[1, c0 + s:c0 + s + 1, :]
        vs = blk_ref[2, c0 + s:c0 + s + 1, :]
        for p in parts:
            sl = slice(p * half, (p + 1) * half)
            d = bc2[sl] - bcs
            if (s // half) == p:
                keep = (row[sl] <= s) if reverse else (row[sl] >= s)
                d = jnp.where(keep, d, -jnp.inf)
            w = jnp.exp2(d) * (q[sl] * ks)
            sc = jnp.sum(w, axis=-1, keepdims=True)
            acc[p] = acc[p] + sc * vs
    return jnp.concatenate(acc, axis=0)


def _split3(x):
    hi = x.astype(BF16)
    r1 = x - hi.astype(F32)
    mid = r1.astype(BF16)
    lo = (r1 - mid.astype(F32)).astype(BF16)
    return hi, mid, lo


def _hgrn_block(blk, reverse, st, aq_ref, ai_ref, af_ref, ls, lb, sums_m, blk_ref):
    bt = LANES
    base = pl.multiple_of(blk * bt, bt)
    rows = pl.ds(base, bt)
    q = _silu(aq_ref[rows, ls])
    v = ai_ref[rows, ls]
    f = lb + (1.0 - lb) * _sigmoid(af_ref[rows, ls])
    k = 1.0 - f
    lf2 = jnp.log(f) * LOG2E
    sums = None
    for piece in _split3(lf2):
        d = jnp.dot(sums_m, piece, preferred_element_type=F32)
        sums = d if sums is None else sums + d
    bc2 = sums[:bt]
    rest2 = sums[bt:]
    blk_ref[0] = bc2
    blk_ref[1] = k
    blk_ref[2] = v
    qe = (q * jnp.exp2(bc2)).astype(BF16)
    ke = (k * jnp.exp2(rest2)).astype(BF16)
    ebt = jnp.exp2(bc2 + rest2)
    vt = v.T.astype(BF16)
    tok = lax.broadcasted_iota(jnp.int32, (bt, 1), 0) // A_CHUNK
    nchunk = bt // A_CHUNK
    kem = jnp.concatenate([jnp.where(tok == c, ke, jnp.zeros_like(ke)) for c in range(nchunk)], axis=1)
    ds_all = jnp.dot(vt, kem, preferred_element_type=F32)
    starts = [None] * nchunk
    order = range(nchunk - 1, -1, -1) if reverse else range(nchunk)
    for c in order:
        starts[c] = st.astype(BF16)
        st = st * ebt[c * A_CHUNK:c * A_CHUNK + 1, :] + ds_all[:, c * A_DK:(c + 1) * A_DK]
    outs = []
    for c in range(nchunk):
        cs = slice(c * A_CHUNK, (c + 1) * A_CHUNK)
        o_inter = _dot_nt(qe[cs], starts[c])
        o_intra = _hgrn_intra(bc2[cs], q[cs], blk_ref, c * A_CHUNK, reverse)
        outs.append(o_inter + o_intra)
    return jnp.concatenate(outs, axis=0), st


def _hgrn_kernel(*refs, seq, has_s0, emit_state):
    aq_ref, ai_ref, aff_ref, afb_ref, ag_ref, lbl_ref, gn_ref = refs[:7]
    pos = 7
    s0_ref = None
    if has_s0:
        s0_ref = refs[pos]
        pos += 1
    o_ref = refs[pos]
    pos += 1
    st_ref = None
    if emit_state:
        st_ref = refs[pos]
        pos += 1
    of_scr, ob_scr = refs[pos], refs[pos + 1]
    blk_scr = refs[pos + 2:pos + 2 + 2 * HGRN_GROUP]

    bt = LANES
    nb = seq // bt
    heads = range(HGRN_GROUP)
    lanes = [slice(g * A_DK, (g + 1) * A_DK) for g in heads]

    def lower_bound(g, d):
        z = lbl_ref[d, g]
        m = jnp.max(z, axis=0, keepdims=True)
        e = jnp.exp(z - m)
        return e[0:1, :] / jnp.sum(e, axis=0, keepdims=True)

    lbs = [(lower_bound(g, 0), lower_bound(g, 1)) for g in heads]

    r = lax.broadcasted_iota(jnp.int32, (bt, bt), 0)
    cidx = lax.broadcasted_iota(jnp.int32, (bt, bt), 1)
    same = (r // A_CHUNK) == (cidx // A_CHUNK)
    one = jnp.ones((bt, bt), F32)
    zero = jnp.zeros((bt, bt), F32)

    def sums_matrix(cum_mask, rest_mask):
        return jnp.concatenate([jnp.where(same & cum_mask, one, zero),
                                jnp.where(same & rest_mask, one, zero)], axis=0).astype(BF16)

    sums_f = sums_matrix(cidx <= r, cidx > r)
    sums_b = sums_matrix(cidx >= r, cidx < r)

    init = []
    for g in heads:
        for d in range(2):
            init.append(s0_ref[d, g].T if has_s0 else jnp.zeros((A_DV, A_DK), F32))

    def body(j, carry):
        carry = list(carry)
        jb = nb - 1 - j
        for g in heads:
            o_f, carry[2 * g] = _hgrn_block(j, False, carry[2 * g], aq_ref, ai_ref, aff_ref, lanes[g],
                                            lbs[g][0], sums_f, blk_scr[2 * g])
            of_scr[pl.ds(pl.multiple_of(j * bt, bt), bt), lanes[g]] = o_f
            o_b, carry[2 * g + 1] = _hgrn_block(jb, True, carry[2 * g + 1], aq_ref, ai_ref, afb_ref, lanes[g],
                                                lbs[g][1], sums_b, blk_scr[2 * g + 1])
            ob_scr[pl.ds(pl.multiple_of(jb * bt, bt), bt), lanes[g]] = o_b
        return tuple(carry)

    final = lax.fori_loop(0, nb, body, tuple(init))

    for g in heads:
        o = of_scr[:, lanes[g]] + ob_scr[:, lanes[g]]
        o = o * lax.rsqrt(jnp.mean(o * o, axis=-1, keepdims=True) + EPS) * gn_ref[g]
        o_ref[:, lanes[g]] = (o * _silu(ag_ref[:, lanes[g]])).astype(o_ref.dtype)
        if emit_state:
            st_ref[0, g] = final[2 * g].T
            st_ref[1, g] = final[2 * g + 1].T


def _hgrn(pa3, lb_logits4, a_norm3, s0, emit_state):
    b, t, _ = pa3.shape
    has_s0 = s0 is not None
    hg = HGRN_GROUP
    ngrp = A_HEADS // hg

    def seg(sidx):
        return pl.BlockSpec((None, t, hg * A_DK), lambda bi, h: (bi, 0, sidx * ngrp + h))

    state_spec = pl.BlockSpec((None, None, 2, hg, A_DK, A_DV), lambda bi, h: (bi, 0, 0, h, 0, 0))
    in_specs = [seg(0), seg(1), seg(2), seg(3), seg(4),
                pl.BlockSpec((2, hg, lb_logits4.shape[2], A_DK), lambda bi, h: (0, h, 0, 0)),
                pl.BlockSpec((hg, 1, A_DV), lambda bi, h: (h, 0, 0))]
    args = [pa3, pa3, pa3, pa3, pa3, lb_logits4, a_norm3]
    if has_s0:
        in_specs.append(state_spec)
        args.append(s0)
    out_specs = [pl.BlockSpec((None, t, hg * A_DV), lambda bi, h: (bi, 0, h))]
    out_shape = [jax.ShapeDtypeStruct((b, t, A_WIDTH), BF16)]
    if emit_state:
        out_specs.append(state_spec)
        out_shape.append(jax.ShapeDtypeStruct((b, 1, 2, A_HEADS, A_DK, A_DV), F32))
    return pl.pallas_call(
        functools.partial(_hgrn_kernel, seq=t, has_s0=has_s0, emit_state=emit_state),
        grid=(b, ngrp),
        in_specs=in_specs,
        out_specs=out_specs,
        out_shape=out_shape,
        scratch_shapes=[pltpu.VMEM((t, hg * A_DV), F32), pltpu.VMEM((t, hg * A_DV), F32)]
        + [pltpu.VMEM((3, LANES, A_DK), F32)] * (2 * hg),
        compiler_params=_cparams(("arbitrary", "arbitrary")),
        name="hgrn_state" if emit_state else "hgrn",
    )(*args)


def _dup_half(x, kv, lane):
    swapped = pltpu.roll(x, B_HEAD_DIM, axis=1)
    first = lane < B_HEAD_DIM
    return jnp.where(first, x, swapped) if kv == 0 else jnp.where(first, swapped, x)


def _attend_heads(q_of_pair, key_sets, sink_ref, o_ref, lane, stack):
    scale = B_HEAD_DIM ** -0.5
    group = B_HEADS // B_KV_HEADS
    first = lane < B_HEAD_DIM
    lq = o_ref.shape[0]
    for kv in range(B_KV_HEADS):
        kvs = [(_dup_half(k, kv, lane).astype(BF16), _dup_half(v, kv, lane).astype(BF16), m)
               for k, v, m in key_sets]
        outs = []
        for h0 in range(0, group, stack):
            qs, sinks = [], []
            for g in range(h0, h0 + stack):
                j = kv * (group // 2) + g // 2
                keep = first if g % 2 == 0 else jnp.logical_not(first)
                qp = q_of_pair(j)
                qs.append(jnp.where(keep, qp, jnp.zeros_like(qp)).astype(BF16))
                sinks.append(jnp.full((lq, 1), sink_ref[kv * group + g], F32))
            qs = jnp.concatenate(qs, axis=0)
            sink = jnp.concatenate(sinks, axis=0)
            ss = []
            for k2, _, m in kvs:
                s = _dot_nt(qs, k2) * scale
                if m is not None:
                    s = jnp.where(jnp.concatenate([m] * stack, axis=0), s, -jnp.inf)
                ss.append(s)
            mx = sink
            for s in ss:
                mx = jnp.maximum(mx, jnp.max(s, axis=-1, keepdims=True))
            es = [jnp.exp(s - mx) for s in ss]
            den = jnp.exp(sink - mx)
            for e in es:
                den = den + jnp.sum(e, axis=-1, keepdims=True)
            inv = 1.0 / den
            og = None
            for e, (_, v2, _) in zip(es, kvs):
                part = jnp.dot((e * inv).astype(BF16), v2, preferred_element_type=F32)
                og = part if og is None else og + part
            outs.extend(og[i * lq:(i + 1) * lq] for i in range(stack))
        for jj in range(group // 2):
            j = kv * (group // 2) + jj
            o_ref[:, j * LANES:(j + 1) * LANES] = jnp.where(first, outs[2 * jj], outs[2 * jj + 1]).astype(o_ref.dtype)


def _ctx_attn_kernel(sink_ref, q_ref, kv_ref, o_ref):
    lane = lax.broadcasted_iota(jnp.int32, (1, LANES), 1)
    kvv = kv_ref[...]
    key_sets = [(kvv[:, :LANES], kvv[:, LANES:], None)]
    _attend_heads(lambda j: q_ref[:, j * LANES:(j + 1) * LANES], key_sets, sink_ref, o_ref, lane, stack=1)


def _ctx_attn(pb3, sink):
    b, t, _ = pb3.shape
    return pl.pallas_call(
        _ctx_attn_kernel,
        grid=(b,),
        in_specs=[pl.BlockSpec(memory_space=pltpu.SMEM),
                  pl.BlockSpec((None, t, B_WIDTH), lambda bi: (bi, 0, 0)),
                  pl.BlockSpec((None, t, 2 * LANES), lambda bi: (bi, 0, B_WIDTH // (2 * LANES)))],
        out_specs=pl.BlockSpec((None, t, B_WIDTH), lambda bi: (bi, 0, 0)),
        out_shape=jax.ShapeDtypeStruct((b, t, B_WIDTH), BF16),
        compiler_params=_cparams(("arbitrary",)),
        name="ctx_attn",
    )(sink, pb3, pb3)


def _rope(x, cos, sin_signed, lane):
    nf = B_HEAD_DIM // 4
    up = pltpu.roll(x, LANES - nf, axis=1)
    down = pltpu.roll(x, nf, axis=1)
    partner = jnp.where((lane % (2 * nf)) < nf, up, down)
    return x * cos + partner * sin_signed


def _lat_attn_kernel(sink_ref, q_ref, kvl_ref, kvc_ref, kvr_ref, ck_ref, cv_ref,
                     cl_ref, cc_ref, cr_ref, sl_ref, sc_ref, sr_ref, o_ref, *, seq):
    n = pl.program_id(1)
    blk = B_BLOCK
    lane = lax.broadcasted_iota(jnp.int32, (1, LANES), 1)
    kw = jnp.concatenate([kvl_ref[:, :LANES], kvc_ref[:, :LANES], kvr_ref[:, :LANES]], axis=0)
    vw = jnp.concatenate([kvl_ref[:, LANES:], kvc_ref[:, LANES:], kvr_ref[:, LANES:]], axis=0)
    cos_w = jnp.concatenate([cl_ref[...], cc_ref[...], cr_ref[...]], axis=0)
    sin_w = jnp.concatenate([sl_ref[...], sc_ref[...], sr_ref[...]], axis=0)
    kw = _rope(kw, cos_w, sin_w, lane)
    qpos = n * blk + lax.broadcasted_iota(jnp.int32, (blk, 3 * blk), 0)
    kpos = (n - 1) * blk + lax.broadcasted_iota(jnp.int32, (blk, 3 * blk), 1)
    mask = (jnp.abs(kpos - qpos) <= WINDOW) & (kpos >= 0) & (kpos < seq)
    key_sets = [(kw, vw, mask), (ck_ref[...], cv_ref[...], None)]
    cos_c = cc_ref[...]
    sin_c = sc_ref[...]

    def q_of_pair(j):
        return _rope(q_ref[:, j * LANES:(j + 1) * LANES], cos_c, sin_c, lane)

    _attend_heads(q_of_pair, key_sets, sink_ref, o_ref, lane, stack=B_HEADS // B_KV_HEADS)


def _lat_attn(pb3, ck, cv, cos_t, sin_t, sink):
    b, t, _ = pb3.shape
    nb = t // B_BLOCK
    lc = ck.shape[1]
    kv_col = B_WIDTH // (2 * LANES)

    def kv_spec(off):
        return pl.BlockSpec((None, B_BLOCK, 2 * LANES),
                            lambda bi, n: (bi, jnp.clip(n + off, 0, nb - 1), kv_col))

    def tab_spec(off):
        return pl.BlockSpec((B_BLOCK, LANES), lambda bi, n: (jnp.clip(n + off, 0, nb - 1), 0))

    return pl.pallas_call(
        functools.partial(_lat_attn_kernel, seq=t),
        grid=(b, nb),
        in_specs=[pl.BlockSpec(memory_space=pltpu.SMEM),
                  pl.BlockSpec((None, B_BLOCK, B_WIDTH), lambda bi, n: (bi, n, 0)),
                  kv_spec(-1), kv_spec(0), kv_spec(1),
                  pl.BlockSpec((None, lc, LANES), lambda bi, n: (bi, 0, 0)),
                  pl.BlockSpec((None, lc, LANES), lambda bi, n: (bi, 0, 0)),
                  tab_spec(-1), tab_spec(0), tab_spec(1),
                  tab_spec(-1), tab_spec(0), tab_spec(1)],
        out_specs=pl.BlockSpec((None, B_BLOCK, B_WIDTH), lambda bi, n: (bi, n, 0)),
        out_shape=jax.ShapeDtypeStruct((b, t, B_WIDTH), BF16),
        compiler_params=_cparams(("arbitrary", "arbitrary")),
        name="lat_attn",
    )(sink, pb3, pb3, pb3, pb3, ck, cv, cos_t, cos_t, cos_t, sin_t, sin_t, sin_t)


def _rope_tables(seq):
    half = B_HEAD_DIM // 2
    nf = half // 2
    row = jnp.repeat(jnp.arange(seq // GRID_W), GRID_W).astype(F32)
    col = jnp.tile(jnp.arange(GRID_W), seq // GRID_W).astype(F32)
    inv = ROPE_BASE ** (-jnp.arange(nf, dtype=F32) / nf)
    ang_r = row[:, None] * inv[None, :]
    ang_c = col[:, None] * inv[None, :]

    def part(ang):
        c = jnp.cos(ang)
        s = jnp.sin(ang)
        return jnp.concatenate([c, c], axis=-1), jnp.concatenate([-s, s], axis=-1)

    cr, sr = part(ang_r)
    cc, sc = part(ang_c)
    cos_h = jnp.concatenate([cr, cc], axis=-1)
    sin_h = jnp.concatenate([sr, sc], axis=-1)
    reps = LANES // B_HEAD_DIM
    return jnp.tile(cos_h, (1, reps)), jnp.tile(sin_h, (1, reps))


def _postmix_kernel(oa_ref, ob_ref, ga_ref, gb_ref, x_ref, gt1_ref, sh2_ref, sc2_ref, g2_ref,
                    woa_ref, wob_ref, wout_ref, wq_ref, keys_ref, x1_ref, h2_ref, s_ref):
    ya = jnp.dot(oa_ref[...], woa_ref[...], preferred_element_type=F32)
    yb = jnp.dot(ob_ref[...], wob_ref[...], preferred_element_type=F32)
    u = _sigmoid(ga_ref[...]) * ya + _sigmoid(gb_ref[...]) * yb
    y = jnp.dot(u.astype(BF16), wout_ref[...], preferred_element_type=F32)
    x1 = x_ref[...] + gt1_ref[...] * y
    x1_ref[...] = x1
    h2 = _rmsnorm(x1, g2_ref[...]) * (1.0 + sc2_ref[...]) + sh2_ref[...]
    h2b = h2.astype(BF16)
    h2_ref[...] = h2b
    q = jnp.dot(h2b, wq_ref[...], preferred_element_type=F32)
    for hp in range(2 * P_HEADS):
        cs = slice(hp * LANES, (hp + 1) * LANES)
        s_ref[hp] = _dot_nt(keys_ref[hp], q[:, cs].astype(BF16))


def _postmix(oa, ob, pg, x2, mod3, g2, woa, wob, wout, wq, keys, row_fn, tm=256):
    t = x2.shape[0]
    nhp, nkeys = keys.shape[0], keys.shape[1]
    full = lambda shape: pl.BlockSpec(shape, lambda i: (0,) * len(shape))
    return pl.pallas_call(
        _postmix_kernel,
        grid=(t // tm,),
        in_specs=[pl.BlockSpec((tm, A_WIDTH), lambda i: (i, 0)),
                  pl.BlockSpec((tm, B_WIDTH), lambda i: (i, 0)),
                  pl.BlockSpec((tm, D_MODEL), lambda i: (i, 0)),
                  pl.BlockSpec((tm, D_MODEL), lambda i: (i, 1)),
                  pl.BlockSpec((tm, D_MODEL), lambda i: (i, 0)),
                  _mod_spec(2, row_fn, tm), _mod_spec(3, row_fn, tm), _mod_spec(4, row_fn, tm),
                  full((1, D_MODEL)),
                  full(woa.shape), full(wob.shape), full(wout.shape), full(wq.shape), full(keys.shape)],
        out_specs=[pl.BlockSpec((tm, D_MODEL), lambda i: (i, 0)),
                   pl.BlockSpec((tm, D_MODEL), lambda i: (i, 0)),
                   pl.BlockSpec((nhp, nkeys, tm), lambda i: (0, 0, i))],
        out_shape=[jax.ShapeDtypeStruct((t, D_MODEL), F32),
                   jax.ShapeDtypeStruct((t, D_MODEL), BF16),
                   jax.ShapeDtypeStruct((nhp, nkeys, t), F32)],
        compiler_params=_cparams(("arbitrary",)),
        name="postmix",
    )(oa, ob, pg, pg, x2, mod3, mod3, mod3, g2, woa, wob, wout, wq, keys)


def _extract_max(vals, order, big):
    m = jnp.max(vals, axis=0, keepdims=True)
    first = jnp.min(jnp.where(vals == m, order, big), axis=0, keepdims=True)
    return m, first, order == first


def _topk_kernel(s_ref, e1_ref, e2_ref, g_ref):
    tl = s_ref.shape[2]
    nk = P_NKEYS
    kk = P_TOPK
    neg = -jnp.inf
    key_iota = lax.broadcasted_iota(jnp.int32, (nk, tl), 0).astype(F32)
    rank = lax.broadcasted_iota(jnp.int32, (kk, tl), 0)
    gs, es = [], []
    for h in range(P_HEADS):
        tops = []
        for side in range(2):
            s = s_ref[2 * h + side]
            v = jnp.zeros((kk, tl), F32)
            ix = jnp.zeros((kk, tl), F32)
            for a in range(kk):
                m, i, hit = _extract_max(s, key_iota, float(nk))
                s = jnp.where(hit, neg, s)
                v = jnp.where(rank == a, m, v)
                ix = jnp.where(rank == a, i, ix)
            tops.append((v, ix))
        (v1, i1), (v2, i2) = tops
        cands, eids, flats = [], [], []
        for a in range(kk // 2):
            nb = kk // (a + 1)
            nr = -(-nb // SUBLANES) * SUBLANES
            r = lax.broadcasted_iota(jnp.int32, (nr, tl), 0)
            c = v1[a:a + 1] + v2[:nr]
            cands.append(c if nb == nr else jnp.where(r < nb, c, neg))
            eids.append(i1[a:a + 1] * float(nk) + i2[:nr])
            flats.append((r + a * kk).astype(F32))
        r = lax.broadcasted_iota(jnp.int32, (kk // 2, tl), 0)
        cands.append(v1[kk // 2:] + v2[0:1])
        eids.append(i1[kk // 2:] * float(nk) + i2[0:1])
        flats.append(((r + kk // 2) * kk).astype(F32))
        cand = jnp.concatenate(cands, axis=0)
        eid = jnp.concatenate(eids, axis=0)
        flat = jnp.concatenate(flats, axis=0)
        ts = jnp.zeros((kk, tl), F32)
        te = jnp.zeros((kk, tl), F32)
        nexp = float(nk * nk)
        okey = flat * nexp + eid
        for k in range(kk):
            m, first, hit = _extract_max(cand, okey, float(kk * kk) * nexp)
            e = first - jnp.floor(first * (1.0 / nexp)) * nexp
            cand = jnp.where(hit, neg, cand)
            ts = jnp.where(rank == k, m, ts)
            te = jnp.where(rank == k, e, te)
        p = jnp.exp(ts - ts[0:1])
        gs.append(p / jnp.sum(p, axis=0, keepdims=True))
        es.append(te)
    g_ref[...] = jnp.concatenate(gs, axis=0).T
    e_all = jnp.concatenate(es, axis=0).T.astype(jnp.int32)
    e1_ref[...] = lax.shift_right_logical(e_all, nk.bit_length() - 1)
    e2_ref[...] = e_all & (nk - 1)


def _topk(scores, tl=128):
    nhp, nkeys, t = scores.shape
    w = P_HEADS * P_TOPK
    return pl.pallas_call(
        _topk_kernel,
        grid=(t // tl,),
        in_specs=[pl.BlockSpec((nhp, nkeys, tl), lambda i: (0, 0, i))],
        out_specs=[pl.BlockSpec((tl, w), lambda i: (i, 0))] * 3,
        out_shape=[jax.ShapeDtypeStruct((t, w), jnp.int32),
                   jax.ShapeDtypeStruct((t, w), jnp.int32),
                   jax.ShapeDtypeStruct((t, w), F32)],
        compiler_params=_cparams(("arbitrary",)),
        name="topk",
    )(scores)


def _gelu(x):
    return 0.5 * x * (1.0 + lax.erf(x * (2.0 ** -0.5)))


def _peer_kernel(h2_ref, e1_ref, e2_ref, g_ref, wut_ref, wv_ref, x1_ref, gt2_ref, nf_ref, o_ref, gate_scr):
    e = pl.program_id(1)
    tm = h2_ref.shape[0]
    nk = P_NKEYS

    @pl.when(e == 0)
    def _build_gates():
        o_ref[...] = jnp.zeros_like(o_ref)
        sub = lax.broadcasted_iota(jnp.int32, (nk, P_HEADS * P_TOPK), 0)

        def per_token(t, carry):
            a = e1_ref[pl.ds(t, 1), :]
            b = e2_ref[pl.ds(t, 1), :]
            w = g_ref[pl.ds(t, 1), :]
            pt = jnp.where(sub == a, w, 0.0).astype(BF16)
            qt = jnp.where(sub == b, 1.0, 0.0).astype(BF16)
            gate_scr[pl.ds(pl.multiple_of(t * G_PITCH, SUBLANES), nk), :] = _dot_nt(pt, qt)
            return carry

        lax.fori_loop(0, tm, per_token, 0, unroll=G_UNROLL)

    h2 = h2_ref[...]
    hids = []
    for c in range(PEER_E1 // 2):
        cols = slice(c * 2 * nk, (c + 1) * 2 * nk)
        a = jnp.dot(h2, wut_ref[:, cols], preferred_element_type=F32)
        gates = [gate_scr[pl.ds(e * PEER_E1 + 2 * c + r, tm, stride=G_PITCH), :] for r in range(2)]
        hids.append((jnp.concatenate(gates, axis=1) * _gelu(a)).astype(BF16))
    o_ref[...] += jnp.dot(jnp.concatenate(hids, axis=1), wv_ref[...], preferred_element_type=F32)

    @pl.when(e == pl.num_programs(1) - 1)
    def _finish():
        x2 = x1_ref[...] + gt2_ref[...] * o_ref[...]
        o_ref[...] = _rmsnorm(x2, nf_ref[...])


def _peer(h2, e1, e2, g, wut, wv, x1, mod3, norm_f, row_fn, tm=PEER_TM):
    t = h2.shape[0]
    te = PEER_E1 * P_NKEYS
    ne = wv.shape[0] // te
    once = pl.Buffered(1)
    return pl.pallas_call(
        _peer_kernel,
        grid=(t // tm, ne),
        in_specs=[pl.BlockSpec((tm, D_MODEL), lambda i, e: (i, 0), pipeline_mode=once),
                  pl.BlockSpec((tm, LANES), lambda i, e: (i, 0), pipeline_mode=once),
                  pl.BlockSpec((tm, LANES), lambda i, e: (i, 0), pipeline_mode=once),
                  pl.BlockSpec((tm, LANES), lambda i, e: (i, 0), pipeline_mode=once),
                  pl.BlockSpec((D_MODEL, te), lambda i, e: (0, e)),
                  pl.BlockSpec((te, D_MODEL), lambda i, e: (e, 0)),
                  pl.BlockSpec((tm, D_MODEL), lambda i, e: (i, 0), pipeline_mode=once),
                  _mod_spec(5, row_fn, tm),
                  pl.BlockSpec((1, D_MODEL), lambda i, e: (0, 0))],
        out_specs=pl.BlockSpec((tm, D_MODEL), lambda i, e: (i, 0)),
        out_shape=jax.ShapeDtypeStruct((t, D_MODEL), F32),
        scratch_shapes=[pltpu.VMEM((tm * G_PITCH, LANES), F32)],
        compiler_params=_cparams(("arbitrary", "arbitrary")),
        name="peer",
    )(h2, e1, e2, g, wut, wv, x1, mod3, norm_f)


def _stream(x, mod3, row_fn, wts, s0, ctx_kv, emit_state):
    b, t, d = x.shape
    x2 = x.reshape(b * t, d)
    pa, pb, pg = _inproj(x2, wts["g1"], mod3, wts["w_in"], row_fn)
    pa3 = pa.reshape(b, t, PA_W)
    pb3 = pb.reshape(b, t, PB_W)
    res = _hgrn(pa3, wts["lb_logits"], wts["a_norm"], s0, emit_state)
    oa = res[0].reshape(b * t, A_WIDTH)
    states = res[1] if emit_state else None
    if ctx_kv is None:
        ob = _ctx_attn(pb3, wts["sink"])
    else:
        cos_t, sin_t = _rope_tables(t)
        ob = _lat_attn(pb3, ctx_kv[0], ctx_kv[1], cos_t, sin_t, wts["sink"])
    ob = ob.reshape(b * t, B_WIDTH)
    x1, h2, scores = _postmix(oa, ob, pg, x2, mod3, wts["g2"], wts["w_oa"], wts["w_ob"], wts["w_out"],
                              wts["p_wq"], wts["p_keys"], row_fn)
    e1, e2, g = _topk(scores)
    y = _peer(h2, e1, e2, g, wts["p_u"], wts["p_v"], x1, mod3, wts["norm_f"], row_fn)
    return y.reshape(b, t, d), states, pb3


def kernel(x_prompt, x_sample, state_hgrn, cache_k, cache_v, c, c_ctx, w_ada, b_ada, norm1, norm2,
           w_in, lb_logits, a_norm, b_sink, w_oa, w_ob, w_out, p_wq, p_keys, p_u, p_v, norm_f):
    depth = w_ada.shape[0]
    assert depth == 1, "single-layer step"
    bp, seq, d = x_prompt.shape
    bs, dseq, _ = x_sample.shape
    kvw = B_KV_HEADS * B_HEAD_DIM

    nrow = -(-(bs + 1) // SUBLANES) * SUBLANES
    cond = jnp.zeros((nrow, d), F32).at[:bs].set(c).at[bs].set(c_ctx)
    mod = _ada(cond, w_ada[0], b_ada[0].reshape(1, -1))
    mod3 = mod.reshape(nrow, 1, 6 * d)

    wts = dict(
        g1=norm1[0].reshape(1, d), g2=norm2[0].reshape(1, d), norm_f=norm_f.reshape(1, d),
        w_in=w_in[0].astype(BF16),
        lb_logits=lb_logits.reshape(2, depth + 1, A_HEADS, A_DK).transpose(0, 2, 1, 3),
        a_norm=a_norm[0].reshape(A_HEADS, 1, A_DV),
        sink=b_sink[0],
        w_oa=w_oa[0].astype(BF16), w_ob=w_ob[0].astype(BF16), w_out=w_out[0].astype(BF16),
        p_wq=p_wq[0].astype(BF16),
        p_keys=p_keys[0].reshape(2 * P_HEADS, P_NKEYS, P_QDIM // 2).astype(BF16),
        p_u=p_u[0].astype(BF16).T, p_v=p_v[0].astype(BF16),
    )

    y_prompt, states, pb3 = _stream(x_prompt, mod3, lambda tok: bs, wts, None, None, True)
    ck = cache_k[:, 0].reshape(bs, -1, kvw)
    cv = cache_v[:, 0].reshape(bs, -1, kvw)
    y_sample, _, _ = _stream(x_sample, mod3, lambda tok: tok // dseq, wts, state_hgrn, (ck, cv), False)

    new_k = pb3[:, :, B_WIDTH:B_WIDTH + kvw].reshape(bp, 1, seq, B_KV_HEADS, B_HEAD_DIM)
    new_v = pb3[:, :, B_WIDTH + kvw:].reshape(bp, 1, seq, B_KV_HEADS, B_HEAD_DIM)
    return (y_prompt, y_sample, states.astype(x_prompt.dtype), new_k, new_v)
```
